```python
import math
import jax
import jax.numpy as jnp
from jax import lax
import numpy as np

D_MODEL = 4096
BATCH = 1
SEQ = 8192
DEPTH = 4

CHUNK = 64
N_A_LAYERS = DEPTH // 2
N_B_LAYERS = DEPTH - N_A_LAYERS
EPS = 1e-6

SSM_EXPAND = 2
D_INNER = SSM_EXPAND * D_MODEL
SSM_HEAD_DIM = 64
SSM_HEADS = D_INNER // SSM_HEAD_DIM
SSM_GROUPS = 8
SSM_HEADS_PER_GROUP = SSM_HEADS // SSM_GROUPS
SSM_STATE = 128
CONV_WIDTH = 4
CONV_DIM = D_INNER + 2 * SSM_GROUPS * SSM_STATE
SSM_IN_DIM = D_INNER + CONV_DIM + SSM_HEADS
SSD_CHUNK = CHUNK
NORM_GROUP = D_INNER // SSM_GROUPS

MLA_HEADS = D_MODEL // 128
Q_RANK = D_MODEL // 4
KV_RANK = 512
NOPE_DIM = 128
ROPE_DIM = 64
QK_DIM = NOPE_DIM + ROPE_DIM
V_DIM = 128
ROPE_THETA = 10000.0
Q_BLOCK = 128

N_GROUPS = 4
EXPERTS_PER_GROUP = 8
N_EXPERTS = N_GROUPS * EXPERTS_PER_GROUP
TOP_K = 2
EXPERT_FF = D_MODEL // 16

kernel_name = "yoco_mamba2_mla_hier_moe_trunk"


def rms_norm(x, g):
    x32 = x.astype(jnp.float32)
    y = x32 * lax.rsqrt(jnp.mean(x32 * x32, axis=-1, keepdims=True) + EPS)
    return (y * g.astype(jnp.float32)).astype(x.dtype)


def rope_tables(positions):
    inv = jnp.power(ROPE_THETA, -jnp.arange(0, ROPE_DIM, 2, dtype=jnp.float32) / ROPE_DIM)
    ang = positions.astype(jnp.float32)[..., None] * inv
    return jnp.cos(ang)[:, :, None, :], jnp.sin(ang)[:, :, None, :]


def apply_rope(t, cos, sin):
    t32 = t.astype(jnp.float32)
    t1, t2 = jnp.split(t32, 2, axis=-1)
    return jnp.concatenate([t1 * cos - t2 * sin, t1 * sin + t2 * cos], axis=-1).astype(t.dtype)


def causal_depthwise_conv(u, w, b):
    y = lax.conv_general_dilated(u, w, window_strides=(1,), padding=[(CONV_WIDTH - 1, 0)],
                                 dimension_numbers=("NWC", "WIO", "NWC"),
                                 feature_group_count=u.shape[-1])
    return y + b


def ssd_chunked_scan(x, dt, a_neg, bmat, cmat):
    bsz, seq = x.shape[:2]
    nc = seq // SSD_CHUNK
    a = dt * a_neg
    xdt = x * dt[..., None]

    def chunks(t):
        return jnp.moveaxis(t.reshape(bsz, nc, SSD_CHUNK, *t.shape[2:]), 1, 0)

    causal = jnp.tril(jnp.ones((SSD_CHUNK, SSD_CHUNK), dtype=bool))[None, :, :, None, None]

    def step(state, inp):
        xc, ac, bc, cc = inp
        acum = jnp.cumsum(ac, axis=1)
        seg = acum[:, :, None] - acum[:, None, :]
        decay = jnp.exp(jnp.where(causal, seg, -jnp.inf))
        cb = jnp.einsum("blgn,bsgn->blsg", cc, bc)
        y_diag = jnp.einsum("blsgh,bsghp->blghp", cb[..., None] * decay, xc)
        y_off = jnp.einsum("blgn,bghpn->blghp", cc, state) * jnp.exp(acum)[..., None]
        to_end = jnp.exp(acum[:, -1:] - acum)
        new_state = (state * jnp.exp(acum[:, -1])[..., None, None]
                     + jnp.einsum("blgn,blghp->bghpn", bc, xc * to_end[..., None]))
        return new_state, y_diag + y_off

    state0 = jnp.zeros((bsz, SSM_GROUPS, SSM_HEADS_PER_GROUP, SSM_HEAD_DIM, SSM_STATE), jnp.float32)
    _, y = lax.scan(step, state0, (chunks(xdt), chunks(a), chunks(bmat), chunks(cmat)))
    return jnp.moveaxis(y, 0, 1).reshape(x.shape)


def mamba2_mixer(h, w_in, conv_w, conv_b, dt_bias, a_log, d_skip, norm_g, w_out):
    bsz, seq, _ = h.shape
    gn = SSM_GROUPS * SSM_STATE
    proj = h @ w_in
    z = proj[..., :D_INNER]
    xbc = proj[..., D_INNER:D_INNER + CONV_DIM]
    dt_raw = proj[..., D_INNER + CONV_DIM:]
    xbc = jax.nn.silu(causal_depthwise_conv(xbc, conv_w, conv_b))
    xs = xbc[..., :D_INNER].astype(jnp.float32).reshape(bsz, seq, SSM_GROUPS, SSM_HEADS_PER_GROUP, SSM_HEAD_DIM)
    bm = xbc[..., D_INNER:D_INNER + gn].astype(jnp.float32).reshape(bsz, seq, SSM_GROUPS, SSM_STATE)
    cm = xbc[..., D_INNER + gn:].astype(jnp.float32).reshape(bsz, seq, SSM_GROUPS, SSM_STATE)
    dt = jax.nn.softplus(dt_raw.astype(jnp.float32) + dt_bias.astype(jnp.float32))
    dt = dt.reshape(bsz, seq, SSM_GROUPS, SSM_HEADS_PER_GROUP)
    a_neg = -jnp.exp(a_log.astype(jnp.float32)).reshape(SSM_GROUPS, SSM_HEADS_PER_GROUP)
    y = ssd_chunked_scan(xs, dt, a_neg, bm, cm)
    y = y + xs * d_skip.astype(jnp.float32).reshape(SSM_GROUPS, SSM_HEADS_PER_GROUP, 1)
    y = y.reshape(bsz, seq, D_INNER) * jax.nn.silu(z.astype(jnp.float32))
    yg = y.reshape(bsz, seq, SSM_GROUPS, NORM_GROUP)
    yg = yg * lax.rsqrt(jnp.mean(yg * yg, axis=-1, keepdims=True) + EPS)
    y = (yg.reshape(bsz, seq, D_INNER) * norm_g.astype(jnp.float32)).astype(h.dtype)
    return y @ w_out


def mla_shared_kv(xs, cos, sin, kv_norm_g, w_dkv, kv_a_norm_g, w_kr, w_uk, w_uv, k_norm_g):
    bsz, seq, _ = xs.shape
    hk = rms_norm(xs, kv_norm_g)
    c_kv = rms_norm(hk @ w_dkv, kv_a_norm_g)
    k_rope = jnp.broadcast_to((hk @ w_kr)[:, :, None, :], (bsz, seq, MLA_HEADS, ROPE_DIM))
    k_nope = (c_kv @ w_uk).reshape(bsz, seq, MLA_HEADS, NOPE_DIM)
    k = rms_norm(jnp.concatenate([k_nope, k_rope], axis=-1), k_norm_g)
    k = jnp.concatenate([k[..., :NOPE_DIM], apply_rope(k[..., NOPE_DIM:], cos, sin)], axis=-1)
    v = (c_kv @ w_uv).reshape(bsz, seq, MLA_HEADS, V_DIM)
    return k, v


def mla_attention(h, k, v, cos, sin, chunk_id, w_dq, q_a_norm_g, w_uq, q_norm_g, w_o):
    bsz, seq, _ = h.shape
    q = rms_norm(h @ w_dq, q_a_norm_g) @ w_uq
    q = rms_norm(q.reshape(bsz, seq, MLA_HEADS, QK_DIM), q_norm_g)
    q = jnp.concatenate([q[..., :NOPE_DIM], apply_rope(q[..., NOPE_DIM:], cos, sin)], axis=-1)
    nb = seq // Q_BLOCK
    qb = jnp.moveaxis(q.reshape(bsz, nb, Q_BLOCK, MLA_HEADS, QK_DIM), 1, 0)
    qc = jnp.moveaxis(chunk_id.reshape(bsz, nb, Q_BLOCK), 1, 0)
    scale = QK_DIM ** -0.5

    def block(args):
        qblk, qcid = args
        s = jnp.einsum("bqhd,bkhd->bhqk", qblk, k).astype(jnp.float32) * scale
        mask = chunk_id[:, None, None, :] <= qcid[:, None, :, None]
        p = jax.nn.softmax(jnp.where(mask, s, -jnp.inf), axis=-1)
        return jnp.einsum("bhqk,bkhd->bqhd", p.astype(v.dtype), v)

    o = lax.map(block, (qb, qc))
    o = jnp.moveaxis(o, 0, 1).reshape(bsz, seq, MLA_HEADS * V_DIM)
    return o @ w_o


def hier_moe(h, w_group, w_expert, w_gate, w_up, w_down):
    bsz, seq, _ = h.shape
    g_logits = (h @ w_group).astype(jnp.float32)
    p_group = jax.nn.softmax(g_logits, axis=-1)
    g_sel = jnp.argmax(g_logits, axis=-1)
    p_sel = jnp.take_along_axis(p_group, g_sel[..., None], axis=-1)[..., 0]
    e_logits = (h @ w_expert).astype(jnp.float32).reshape(bsz, seq, N_GROUPS, EXPERTS_PER_GROUP)
    e_in_group = jnp.take_along_axis(e_logits, g_sel[..., None, None], axis=2)[:, :, 0]
    top_v, top_i = lax.top_k(e_in_group, TOP_K)
    w_sel = jax.nn.softmax(top_v, axis=-1) * p_sel[..., None]
    e_idx = g_sel[..., None] * EXPERTS_PER_GROUP + top_i
    gates = jnp.sum(jax.nn.one_hot(e_idx, N_EXPERTS, dtype=jnp.float32) * w_sel[..., None], axis=-2)
    hid = jax.nn.silu(jnp.einsum("bsd,edf->bsef", h, w_gate)) * jnp.einsum("bsd,edf->bsef", h, w_up)
    hid = hid * gates.astype(hid.dtype)[..., None]
    return jnp.einsum("bsef,efd->bsd", hid, w_down)


def setup_inputs(seed: int = 0) -> dict:
    key = jax.random.key(seed)
    ks = iter(jax.random.split(key, 40))

    def nrm(shape, scale):
        return jax.random.normal(next(ks), shape, jnp.float32) * scale

    def gain(shape):
        return 1.0 + nrm(shape, 0.02)

    x = jax.random.normal(next(ks), (BATCH, SEQ, D_MODEL), jnp.float32)
    positions = jnp.broadcast_to(jnp.arange(SEQ, dtype=jnp.int32)[None, :], (BATCH, SEQ))
    dt0 = jnp.exp(jax.random.uniform(next(ks), (N_A_LAYERS, SSM_HEADS), jnp.float32)
                  * (math.log(0.1) - math.log(0.001)) + math.log(0.001))
    return {
        "x": x,
        "positions": positions,
        "g_mix": gain((DEPTH, D_MODEL)),
        "g_ffn": gain((DEPTH, D_MODEL)),
        "ssm_w_in": nrm((N_A_LAYERS, D_MODEL, SSM_IN_DIM), D_MODEL ** -0.5),
        "ssm_conv_w": nrm((N_A_LAYERS, CONV_WIDTH, 1, CONV_DIM), CONV_WIDTH ** -0.5),
        "ssm_conv_b": nrm((N_A_LAYERS, CONV_DIM), 0.02),
        "ssm_dt_bias": dt0 + jnp.log(-jnp.expm1(-dt0)),
        "ssm_a_log": jnp.log(jax.random.uniform(next(ks), (N_A_LAYERS, SSM_HEADS), jnp.float32, 1.0, 16.0)),
        "ssm_d": 1.0 + nrm((N_A_LAYERS, SSM_HEADS), 0.1),
        "ssm_norm_g": gain((N_A_LAYERS, D_INNER)),
        "ssm_w_out": nrm((N_A_LAYERS, D_INNER, D_MODEL), D_INNER ** -0.5),
        "kv_norm_g": gain((D_MODEL,)),
        "w_dkv": nrm((D_MODEL, KV_RANK), D_MODEL ** -0.5),
        "kv_a_norm_g": gain((KV_RANK,)),
        "w_kr": nrm((D_MODEL, ROPE_DIM), D_MODEL ** -0.5),
        "w_uk": nrm((KV_RANK, MLA_HEADS * NOPE_DIM), KV_RANK ** -0.5),
        "w_uv": nrm((KV_RANK, MLA_HEADS * V_DIM), KV_RANK ** -0.5),
        "k_norm_g": gain((QK_DIM,)),
        "q_w_dq": nrm((N_B_LAYERS, D_MODEL, Q_RANK), D_MODEL ** -0.5),
        "q_a_norm_g": gain((N_B_LAYERS, Q_RANK)),
        "q_w_uq": nrm((N_B_LAYERS, Q_RANK, MLA_HEADS * QK_DIM), Q_RANK ** -0.5),
        "q_norm_g": gain((N_B_LAYERS, QK_DIM)),
        "attn_w_o": nrm((N_B_LAYERS, MLA_HEADS * V_DIM, D_MODEL), (MLA_HEADS * V_DIM) ** -0.5),
        "moe_w_group": nrm((DEPTH, D_MODEL, N_GROUPS), D_MODEL ** -0.5),
        "moe_w_expert": nrm((DEPTH, D_MODEL, N_EXPERTS), D_MODEL ** -0.5),
        "moe_w_gate": nrm((DEPTH, N_EXPERTS, D_MODEL, EXPERT_FF), D_MODEL ** -0.5),
        "moe_w_up": nrm((DEPTH, N_EXPERTS, D_MODEL, EXPERT_FF), D_MODEL ** -0.5),
        "moe_w_down": nrm((DEPTH, N_EXPERTS, EXPERT_FF, D_MODEL), EXPERT_FF ** -0.5),
    }


def reference(x, positions, g_mix, g_ffn, ssm_w_in, ssm_conv_w, ssm_conv_b, ssm_dt_bias,
              ssm_a_log, ssm_d, ssm_norm_g, ssm_w_out, kv_norm_g, w_dkv, kv_a_norm_g, w_kr,
              w_uk, w_uv, k_norm_g, q_w_dq, q_a_norm_g, q_w_uq, q_norm_g, attn_w_o,
              moe_w_group, moe_w_expert, moe_w_gate, moe_w_up, moe_w_down):
    cos, sin = rope_tables(positions)
    chunk_id = positions // CHUNK
    k_shared = None
    v_shared = None
    for layer in range(DEPTH):
        h = rms_norm(x, g_mix[layer])
        if layer < N_A_LAYERS:
            i = layer
            x = x + mamba2_mixer(h, ssm_w_in[i], ssm_conv_w[i], ssm_conv_b[i], ssm_dt_bias[i],
                                 ssm_a_log[i], ssm_d[i], ssm_norm_g[i], ssm_w_out[i])
        else:
            j = layer - N_A_LAYERS
            x = x + mla_attention(h, k_shared, v_shared, cos, sin, chunk_id, q_w_dq[j],
                                  q_a_norm_g[j], q_w_uq[j], q_norm_g[j], attn_w_o[j])
        x = x + hier_moe(rms_norm(x, g_ffn[layer]), moe_w_group[layer], moe_w_expert[layer],
                         moe_w_gate[layer], moe_w_up[layer], moe_w_down[layer])
        if layer == N_A_LAYERS - 1:
            k_shared, v_shared = mla_shared_kv(x, cos, sin, kv_norm_g, w_dkv, kv_a_norm_g,
                                               w_kr, w_uk, w_uv, k_norm_g)
    return x
```

```python
import functools

import jax
import jax.numpy as jnp
from jax import lax
from jax.experimental import pallas as pl
from jax.experimental.pallas import tpu as pltpu

EPS = 1e-6
CHUNK = 64
CHUNK_SHIFT = 6

SSM_HEAD_DIM = 64
SSM_GROUPS = 8
SSM_STATE = 128
CONV_WIDTH = 4
SSD_L = 128
CONV_PAD = 8

NOPE_DIM = 128
ROPE_DIM = 64
QK_DIM = NOPE_DIM + ROPE_DIM
V_DIM = 128
ROPE_THETA = 10000.0

N_GROUPS = 4
EXPERTS_PER_GROUP = 8
N_EXPERTS = N_GROUPS * EXPERTS_PER_GROUP
TOP_K = 2
MOE_TM = 256

LANES = 128
VMEM_LIMIT_BYTES = 56 * 1024 * 1024
NEG_BIG = -1e30


def _cparams(sem):
    return pltpu.CompilerParams(dimension_semantics=sem, vmem_limit_bytes=VMEM_LIMIT_BYTES)


def _silu(v):
    return v * (1.0 / (1.0 + jnp.exp(-v)))


def _softplus(v):
    return jnp.maximum(v, 0.0) + jnp.log1p(jnp.exp(-jnp.abs(v)))


def _bf16(v):
    return v.astype(jnp.bfloat16)


def _dot(a, b):
    return jnp.dot(a, b, preferred_element_type=jnp.float32)


def _split3(v):
    hi = _bf16(v)
    r1 = v - hi.astype(jnp.float32)
    mid = _bf16(r1)
    lo = _bf16(r1 - mid.astype(jnp.float32))
    return hi, mid, lo


def _prenorm_kernel(x_ref, g_ref, o_ref):
    x = x_ref[...]
    y = x * lax.rsqrt(jnp.mean(x * x, axis=-1, keepdims=True) + EPS)
    o_ref[...] = _bf16(y * g_ref[...])


def _prenorm(x, g, tm=256):
    s, d = x.shape
    return pl.pallas_call(
        _prenorm_kernel,
        out_shape=jax.ShapeDtypeStruct((s, d), jnp.bfloat16),
        grid=(s // tm,),
        in_specs=[pl.BlockSpec((tm, d), lambda i: (i, 0)),
                  pl.BlockSpec((1, d), lambda i: (0, 0))],
        out_specs=pl.BlockSpec((tm, d), lambda i: (i, 0)),
        compiler_params=_cparams(("parallel",)),
        name="prenorm",
    )(x, g.reshape(1, d))


def _mm_kernel(*refs, nk, mode):
    a_ref, w_ref = refs[0], refs[1]
    if mode == "plain":
        extra, o_ref, rest = None, refs[2], refs[3:]
    else:
        extra, o_ref, rest = refs[2], refs[3], refs[4:]

    def finish(acc):
        if mode == "residual":
            acc = acc + extra[...]
        elif mode == "rmsnorm":
            acc = acc * lax.rsqrt(jnp.mean(acc * acc, axis=-1, keepdims=True) + EPS) * extra[...]
        o_ref[...] = acc.astype(o_ref.dtype)

    if nk == 1:
        finish(_dot(a_ref[...], w_ref[...]))
        return

    acc_ref = rest[0]
    k = pl.program_id(2)

    @pl.when(k == 0)
    def _():
        acc_ref[...] = jnp.zeros_like(acc_ref)

    acc_ref[...] += _dot(a_ref[...], w_ref[...])

    @pl.when(k == nk - 1)
    def _():
        finish(acc_ref[...])


def _mm(a, w, *, out_dtype, tm, tn, tk=None, mode="plain", extra=None, name="mm"):
    m, kdim = a.shape
    n = w.shape[1]
    tm = min(tm, m)
    tk = kdim if tk is None else tk
    nk = kdim // tk
    in_specs = [pl.BlockSpec((tm, tk), lambda i, j, k: (i, k)),
                pl.BlockSpec((tk, tn), lambda i, j, k: (k, j))]
    args = [a, w]
    if mode == "residual":
        in_specs.append(pl.BlockSpec((tm, tn), lambda i, j, k: (i, j)))
        args.append(extra)
    elif mode == "rmsnorm":
        assert tn == n
        in_specs.append(pl.BlockSpec((1, tn), lambda i, j, k: (0, j)))
        args.append(extra.reshape(1, n))
    scratch = [pltpu.VMEM((tm, tn), jnp.float32)] if nk > 1 else []
    return pl.pallas_call(
        functools.partial(_mm_kernel, nk=nk, mode=mode),
        out_shape=jax.ShapeDtypeStruct((m, n), out_dtype),
        grid=(m // tm, n // tn, nk),
        in_specs=in_specs,
        out_specs=pl.BlockSpec((tm, tn), lambda i, j, k: (i, j)),
        scratch_shapes=scratch,
        compiler_params=_cparams(("parallel", "parallel", "arbitrary")),
        name=name,
    )(*args)


def _pair_expand(cols, j, lane_lo):
    return jnp.where(lane_lo, cols[:, 2 * j:2 * j + 1], cols[:, 2 * j + 1:2 * j + 2])


def _ssd_kernel(z_ref, x_ref, b_ref, c_ref, cwx_ref, cwb_ref, cwc_ref, cbx_ref, cbb_ref, cbc_ref,
                dtc_ref, dtr_ref, biasc_ref, biasr_ref, alogc_ref, alogr_ref, dskip_ref, ng_ref,
                o_ref, xbuf, bbuf, cbuf, state):
    L = SSD_L
    c = pl.program_id(1)

    @pl.when(c == 0)
    def _():
        xbuf[0:CONV_PAD, :] = jnp.zeros((CONV_PAD, xbuf.shape[1]), jnp.float32)
        bbuf[0:CONV_PAD, :] = jnp.zeros((CONV_PAD, bbuf.shape[1]), jnp.float32)
        cbuf[0:CONV_PAD, :] = jnp.zeros((CONV_PAD, cbuf.shape[1]), jnp.float32)
        state[...] = jnp.zeros_like(state)

    def conv_silu(src_ref, buf, w_ref, bias_ref):
        buf[CONV_PAD:CONV_PAD + L, :] = src_ref[...].astype(jnp.float32)
        acc = bias_ref[...] + w_ref[CONV_WIDTH - 1:CONV_WIDTH, :] * buf[CONV_PAD:CONV_PAD + L, :]
        for k in range(CONV_WIDTH - 1):
            off = CONV_PAD - (CONV_WIDTH - 1) + k
            acc = acc + w_ref[k:k + 1, :] * buf[off:off + L, :]
        buf[0:CONV_PAD, :] = buf[L:L + CONV_PAD, :]
        return _silu(acc)

    xs = conv_silu(x_ref, xbuf, cwx_ref, cbx_ref)
    bs = conv_silu(b_ref, bbuf, cwb_ref, cbb_ref)
    cs = conv_silu(c_ref, cbuf, cwc_ref, cbc_ref)
    bs16 = _bf16(bs)
    cs16 = _bf16(cs)

    dtc = _softplus(dtc_ref[...] + biasc_ref[...])
    dtr = _softplus(dtr_ref[...] + biasr_ref[...])
    a_c = dtc * (-jnp.exp(alogc_ref[...]))
    a_r = dtr * (-jnp.exp(alogr_ref[...]))

    rid = lax.broadcasted_iota(jnp.int32, (L, L), 0)
    cid = lax.broadcasted_iota(jnp.int32, (L, L), 1)
    causal = rid >= cid
    tril = _bf16(jnp.where(causal, 1.0, 0.0))
    triu = _bf16(jnp.where(rid <= cid, 1.0, 0.0))
    acum_c = sum(_dot(tril, t) for t in _split3(a_c))
    acum_r = sum(_dot(t, triu) for t in _split3(a_r))

    total_c = acum_c[L - 1:L, :]
    e_c = jnp.exp(acum_c)
    w_c = dtc * jnp.exp(total_c - acum_c)
    etot = jnp.exp(total_c)

    cb = lax.dot_general(cs16, bs16, (((1,), (1,)), ((), ())),
                         preferred_element_type=jnp.float32)
    st = state[...]
    yoff = _dot(cs16, _bf16(st))

    lane = lax.broadcasted_iota(jnp.int32, (L, LANES), 1)
    lane_lo = lane < SSM_HEAD_DIM
    lane1_lo = lax.broadcasted_iota(jnp.int32, (1, LANES), 1) < SSM_HEAD_DIM

    z = z_ref[...].astype(jnp.float32)
    n_pairs = xs.shape[1] // LANES
    ys = []
    xws = []
    for j in range(n_pairs):
        xp = xs[:, j * LANES:(j + 1) * LANES]
        ms = []
        for hh in (2 * j, 2 * j + 1):
            seg = acum_c[:, hh:hh + 1] - acum_r[hh:hh + 1, :]
            dec = jnp.where(causal, jnp.exp(seg), 0.0)
            ms.append(_bf16(cb * dec * dtr[hh:hh + 1, :]))
        mpair = jnp.concatenate(ms, axis=1)
        rhs = jnp.concatenate([jnp.where(lane_lo, xp, 0.0), jnp.where(lane_lo, 0.0, xp)], axis=0)
        ydiag = _dot(mpair, _bf16(rhs))
        y = ydiag + yoff[:, j * LANES:(j + 1) * LANES] * _pair_expand(e_c, j, lane_lo)
        y = y + xp * dskip_ref[:, j * LANES:(j + 1) * LANES]
        ys.append(y)
        xws.append(_bf16(xp * _pair_expand(w_c, j, lane_lo)))
    y = jnp.concatenate(ys, axis=1)
    xw = jnp.concatenate(xws, axis=1)

    snew = lax.dot_general(bs16, xw, (((0,), (0,)), ((), ())),
                           preferred_element_type=jnp.float32)
    dec_row = jnp.concatenate([_pair_expand(etot, j, lane1_lo) for j in range(n_pairs)], axis=1)
    state[...] = st * dec_row + snew

    y = y * _silu(z)
    y = y * lax.rsqrt(jnp.mean(y * y, axis=-1, keepdims=True) + EPS)
    o_ref[...] = _bf16(y * ng_ref[...])


def _ssd(proj, dt_raw, conv_w, conv_b, dt_bias, a_log, d_skip, norm_g):
    s = proj.shape[0]
    g = SSM_GROUPS
    heads = dt_raw.shape[1]
    hg = heads // g
    gw = hg * SSM_HEAD_DIM
    d_inner = g * gw
    n = SSM_STATE
    L = SSD_L
    xb = d_inner // gw
    bb = 2 * d_inner // n
    cb_ = bb + g
    cwb = d_inner // n
    cwc = cwb + g

    dtc = dt_raw.reshape(s, g, hg).transpose(1, 0, 2)
    dtr = dt_raw.reshape(s, g, hg).transpose(1, 2, 0)
    cw = conv_w.reshape(CONV_WIDTH, -1)
    cbias = conv_b.reshape(1, -1)
    dsk = jnp.repeat(d_skip.reshape(g, 1, hg), SSM_HEAD_DIM, axis=2)

    def gspec(shape, fn):
        return pl.BlockSpec(shape, fn)

    in_specs = [
        gspec((L, gw), lambda gi, c: (c, gi)),
        gspec((L, gw), lambda gi, c: (c, xb + gi)),
        gspec((L, n), lambda gi, c: (c, bb + gi)),
        gspec((L, n), lambda gi, c: (c, cb_ + gi)),
        gspec((CONV_WIDTH, gw), lambda gi, c: (0, gi)),
        gspec((CONV_WIDTH, n), lambda gi, c: (0, cwb + gi)),
        gspec((CONV_WIDTH, n), lambda gi, c: (0, cwc + gi)),
        gspec((1, gw), lambda gi, c: (0, gi)),
        gspec((1, n), lambda gi, c: (0, cwb + gi)),
        gspec((1, n), lambda gi, c: (0, cwc + gi)),
        gspec((None, L, hg), lambda gi, c: (gi, c, 0)),
        gspec((None, hg, L), lambda gi, c: (gi, 0, c)),
        gspec((None, 1, hg), lambda gi, c: (gi, 0, 0)),
        gspec((None, hg, 1), lambda gi, c: (gi, 0, 0)),
        gspec((None, 1, hg), lambda gi, c: (gi, 0, 0)),
        gspec((None, hg, 1), lambda gi, c: (gi, 0, 0)),
        gspec((None, 1, gw), lambda gi, c: (gi, 0, 0)),
        gspec((1, gw), lambda gi, c: (0, gi)),
    ]
    return pl.pallas_call(
        _ssd_kernel,
        out_shape=jax.ShapeDtypeStruct((s, d_inner), jnp.bfloat16),
        grid=(g, s // L),
        in_specs=in_specs,
        out_specs=pl.BlockSpec((L, gw), lambda gi, c: (c, gi)),
        scratch_shapes=[pltpu.VMEM((CONV_PAD + L, gw), jnp.float32),
                        pltpu.VMEM((CONV_PAD + L, n), jnp.float32),
                        pltpu.VMEM((CONV_PAD + L, n), jnp.float32),
                        pltpu.VMEM((n, gw), jnp.float32)],
        compiler_params=_cparams(("parallel", "arbitrary")),
        name="ssd",
    )(proj, proj, proj, proj, cw, cw, cw, cbias, cbias, cbias, dtc, dtr,
      dt_bias.reshape(g, 1, hg), dt_bias.reshape(g, hg, 1),
      a_log.reshape(g, 1, hg), a_log.reshape(g, hg, 1), dsk, norm_g.reshape(1, d_inner))


def _mamba_layer(x, g_mix, w_in, conv_w, conv_b, dt_bias, a_log, d_skip, norm_g, w_out):
    d_inner = w_out.shape[0]
    n_main = 2 * d_inner + 2 * SSM_GROUPS * SSM_STATE
    h = _prenorm(x, g_mix)
    w16 = _bf16(w_in)
    proj = _mm(h, w16[:, :n_main], out_dtype=jnp.bfloat16, tm=1024, tn=1024, name="ssm_in_proj")
    dt_raw = _mm(h, w16[:, n_main:], out_dtype=jnp.float32, tm=1024, tn=w_in.shape[1] - n_main,
                 name="ssm_dt_proj")
    y = _ssd(proj, dt_raw, conv_w, conv_b, dt_bias, a_log, d_skip, norm_g)
    return _mm(y, _bf16(w_out), out_dtype=jnp.float32, tm=1024, tn=1024, tk=2048,
               mode="residual", extra=x, name="ssm_out_proj")


def _router_kernel(x_ref, g_ref, whi_ref, wlo_ref, oi_ref, of_ref, cnt_ref, carry):
    i = pl.program_id(0)
    tm = x_ref.shape[0]

    @pl.when(i == 0)
    def _():
        carry[...] = jnp.zeros_like(carry)

    x = x_ref[...]
    h = x * lax.rsqrt(jnp.mean(x * x, axis=-1, keepdims=True) + EPS) * g_ref[...]
    hhi = _bf16(h)
    hlo = _bf16(h - hhi.astype(jnp.float32))
    whi = whi_ref[...]
    logits = _dot(hhi, whi) + (_dot(hlo, whi) + _dot(hhi, wlo_ref[...]))

    lane = lax.broadcasted_iota(jnp.int32, (tm, LANES), 1)
    ninf = -jnp.inf
    gmask = lane < N_GROUPS
    gl = jnp.where(gmask, logits, ninf)
    gmax = jnp.max(gl, axis=-1, keepdims=True)
    gsel = jnp.min(jnp.where(gl == gmax, lane, LANES), axis=-1, keepdims=True)
    psel = 1.0 / jnp.sum(jnp.where(gmask, jnp.exp(logits - gmax), 0.0), axis=-1, keepdims=True)
    lo = N_GROUPS + EXPERTS_PER_GROUP * gsel
    emask = (lane >= lo) & (lane < lo + EXPERTS_PER_GROUP)
    el = jnp.where(emask, logits, ninf)
    v1 = jnp.max(el, axis=-1, keepdims=True)
    i1 = jnp.min(jnp.where(el == v1, lane, LANES), axis=-1, keepdims=True)
    el2 = jnp.where(lane == i1, ninf, el)
    v2 = jnp.max(el2, axis=-1, keepdims=True)
    i2 = jnp.min(jnp.where(el2 == v2, lane, LANES), axis=-1, keepdims=True)
    e21 = jnp.exp(v2 - v1)
    w1 = psel / (1.0 + e21)
    w2 = psel * e21 / (1.0 + e21)

    hit1 = lane == i1
    hit2 = lane == i2
    onehot = jnp.where(hit1 | hit2, 1.0, 0.0)
    rid = lax.broadcasted_iota(jnp.int32, (tm, tm), 0)
    cid = lax.broadcasted_iota(jnp.int32, (tm, tm), 1)
    strict = _bf16(jnp.where(rid > cid, 1.0, 0.0))
    before = _dot(strict, _bf16(onehot)) + carry[...]
    r1 = jnp.sum(jnp.where(hit1, before, 0.0), axis=-1, keepdims=True)
    r2 = jnp.sum(jnp.where(hit2, before, 0.0), axis=-1, keepdims=True)
    carry[...] += jnp.sum(onehot, axis=0, keepdims=True)

    e1 = i1 - N_GROUPS
    e2 = i2 - N_GROUPS
    oi = jnp.where(lane == 0, e1, jnp.where(lane == 1, e2, jnp.where(
        lane == 2, r1.astype(jnp.int32), jnp.where(lane == 3, r2.astype(jnp.int32), 0))))
    oi_ref[...] = oi
    of_ref[...] = jnp.where(lane == 0, w1, jnp.where(lane == 1, w2, 0.0))
    cnt_ref[...] = carry[...]


def _router(x, g_ffn, w_group, w_expert, tm=256):
    s, d = x.shape
    wr = jnp.concatenate([w_group, w_expert], axis=1)
    wr = jnp.pad(wr, ((0, 0), (0, LANES - wr.shape[1])))
    whi = _bf16(wr)
    wlo = _bf16(wr - whi.astype(jnp.float32))
    return pl.pallas_call(
        _router_kernel,
        out_shape=(jax.ShapeDtypeStruct((s, LANES), jnp.int32),
                   jax.ShapeDtypeStruct((s, LANES), jnp.float32),
                   jax.ShapeDtypeStruct((1, LANES), jnp.float32)),
        grid=(s // tm,),
        in_specs=[pl.BlockSpec((tm, d), lambda i: (i, 0)),
                  pl.BlockSpec((1, d), lambda i: (0, 0)),
                  pl.BlockSpec((d, LANES), lambda i: (0, 0)),
                  pl.BlockSpec((d, LANES), lambda i: (0, 0))],
        out_specs=(pl.BlockSpec((tm, LANES), lambda i: (i, 0)),
                   pl.BlockSpec((tm, LANES), lambda i: (i, 0)),
                   pl.BlockSpec((1, LANES), lambda i: (0, 0))),
        scratch_shapes=[pltpu.VMEM((1, LANES), jnp.float32)],
        compiler_params=_cparams(("arbitrary",)),
        name="moe_router",
    )(x, g_ffn.reshape(1, d), whi, wlo)


def _dispatch_kernel(pos_ref, x_hbm, xs_in, xs_hbm, sem, *, tb):
    del xs_in
    base = pl.program_id(0) * tb

    def row_copy(t, slot):
        p = pos_ref[2 * (base + t) + slot]
        return pltpu.make_async_copy(x_hbm.at[pl.ds(base + t, 1)], xs_hbm.at[pl.ds(p, 1)], sem)

    def issue(t, carry):
        row_copy(t, 0).start()
        row_copy(t, 1).start()
        return carry

    def drain(t, carry):
        row_copy(t, 0).wait()
        row_copy(t, 1).wait()
        return carry

    lax.fori_loop(0, tb, issue, 0)
    lax.fori_loop(0, tb, drain, 0)


def _dispatch(x, pos_flat, n_rows, tb=512):
    s, d = x.shape
    xs0 = jnp.zeros((n_rows, d), x.dtype)
    return pl.pallas_call(
        functools.partial(_dispatch_kernel, tb=tb),
        out_shape=jax.ShapeDtypeStruct((n_rows, d), x.dtype),
        grid_spec=pltpu.PrefetchScalarGridSpec(
            num_scalar_prefetch=1,
            grid=(s // tb,),
            in_specs=[pl.BlockSpec(memory_space=pl.ANY), pl.BlockSpec(memory_space=pl.ANY)],
            out_specs=pl.BlockSpec(memory_space=pl.ANY),
            scratch_shapes=[pltpu.SemaphoreType.DMA(())]),
        input_output_aliases={2: 0},
        compiler_params=pltpu.CompilerParams(dimension_semantics=("arbitrary",),
                                             has_side_effects=True),
        name="moe_dispatch",
    )(pos_flat, x, xs0)


def _expert_kernel(te_ref, tv_ref, xs_ref, g_ref, wg_ref, wu_ref, wd_ref, ys_ref):
    del te_ref
    i = pl.program_id(0)

    @pl.when(tv_ref[i] > 0)
    def _():
        x = xs_ref[...]
        h = _bf16(x * lax.rsqrt(jnp.mean(x * x, axis=-1, keepdims=True) + EPS) * g_ref[...])
        gate = _dot(h, _bf16(wg_ref[...]))
        up = _dot(h, _bf16(wu_ref[...]))
        hid = _bf16(_silu(gate) * up)
        ys_ref[...] = _dot(hid, _bf16(wd_ref[...]))

    @pl.when(tv_ref[i] == 0)
    def _():
        ys_ref[...] = jnp.zeros_like(ys_ref)


def _experts(xs, g_ffn, w_gate, w_up, w_down, layer, tile_e, tile_v):
    n_rows, d = xs.shape
    ff = w_gate.shape[-1]
    nt = n_rows // MOE_TM
    return pl.pallas_call(
        _expert_kernel,
        out_shape=jax.ShapeDtypeStruct((n_rows, d), jnp.float32),
        grid_spec=pltpu.PrefetchScalarGridSpec(
            num_scalar_prefetch=2,
            grid=(nt,),
            in_specs=[pl.BlockSpec((MOE_TM, d), lambda i, te, tv: (i, 0)),
                      pl.BlockSpec((1, d), lambda i, te, tv: (0, 0)),
                      pl.BlockSpec((None, None, d, ff), lambda i, te, tv: (layer, te[i], 0, 0)),
                      pl.BlockSpec((None, None, d, ff), lambda i, te, tv: (layer, te[i], 0, 0)),
                      pl.BlockSpec((None, None, ff, d), lambda i, te, tv: (layer, te[i], 0, 0))],
            out_specs=pl.BlockSpec((MOE_TM, d), lambda i, te, tv: (i, 0))),
        compiler_params=_cparams(("arbitrary",)),
        name="moe_experts",
    )(tile_e, tile_v, xs, g_ffn.reshape(1, d), w_gate, w_up, w_down)


def _combine_kernel(pos_ref, x_ref, wt_ref, ys_hbm, o_ref, buf, sem, *, tc):
    base = pl.program_id(0) * tc

    def row_copy(t, slot):
        p = pos_ref[2 * (base + t) + slot]
        return pltpu.make_async_copy(ys_hbm.at[pl.ds(p, 1)], buf.at[slot, pl.ds(t, 1)], sem)

    def issue(t, carry):
        row_copy(t, 0).start()
        row_copy(t, 1).start()
        return carry

    def drain(t, carry):
        row_copy(t, 0).wait()
        row_copy(t, 1).wait()
        return carry

    lax.fori_loop(0, tc, issue, 0)
    lax.fori_loop(0, tc, drain, 0)
    wt = wt_ref[...]
    o_ref[...] = x_ref[...] + wt[:, 0:1] * buf[0] + wt[:, 1:2] * buf[1]


def _combine(x, wts, ys, pos_flat, tc=256):
    s, d = x.shape
    return pl.pallas_call(
        functools.partial(_combine_kernel, tc=tc),
        out_shape=jax.ShapeDtypeStruct((s, d), jnp.float32),
        grid_spec=pltpu.PrefetchScalarGridSpec(
            num_scalar_prefetch=1,
            grid=(s // tc,),
            in_specs=[pl.BlockSpec((tc, d), lambda i, pos: (i, 0)),
                      pl.BlockSpec((tc, LANES), lambda i, pos: (i, 0)),
                      pl.BlockSpec(memory_space=pl.ANY)],
            out_specs=pl.BlockSpec((tc, d), lambda i, pos: (i, 0)),
            scratch_shapes=[pltpu.VMEM((TOP_K, tc, d), jnp.float32),
                            pltpu.SemaphoreType.DMA(())]),
        compiler_params=_cparams(("arbitrary",)),
        name="moe_combine",
    )(pos_flat, x, wts, ys)


def _moe_layer(x, g_ffn, w_group, w_expert, w_gate, w_up, w_down, layer):
    s = x.shape[0]
    oi, wts, cnt = _router(x, g_ffn, w_group, w_expert)
    e = oi[:, 0:TOP_K]
    rank = oi[:, TOP_K:2 * TOP_K]
    counts = cnt[0, N_GROUPS:N_GROUPS + N_EXPERTS].astype(jnp.int32)
    tiles = (counts + MOE_TM - 1) // MOE_TM
    tile_end = jnp.cumsum(tiles)
    row_off = (tile_end - tiles) * MOE_TM
    pos = (row_off[e] + rank).reshape(-1).astype(jnp.int32)
    nt = s * TOP_K // MOE_TM + N_EXPERTS
    tidx = jnp.minimum(jnp.arange(nt, dtype=jnp.int32), tile_end[-1] - 1)
    tile_e = jnp.minimum(jnp.searchsorted(tile_end, tidx, side="right"), N_EXPERTS - 1)
    tile_v = (jnp.arange(nt, dtype=jnp.int32) < tile_end[-1]).astype(jnp.int32)
    xs = _dispatch(x, pos, nt * MOE_TM)
    ys = _experts(xs, g_ffn, w_gate, w_up, w_down, layer, tile_e.astype(jnp.int32), tile_v)
    return _combine(x, wts, ys, pos)


def _rope_table_kernel(pos_ref, inv_ref, cos_ref, sin_ref):
    ang = pos_ref[...].astype(jnp.float32) * inv_ref[...]
    cos_ref[...] = jnp.cos(ang)
    sin_ref[...] = jnp.sin(ang)


def _rope_tables(pos_col, tm=512):
    s = pos_col.shape[0]
    half = ROPE_DIM // 2
    inv = jnp.power(ROPE_THETA, -jnp.arange(0, ROPE_DIM, 2, dtype=jnp.float32) / ROPE_DIM)
    inv = jnp.concatenate([inv, inv, jnp.zeros((LANES - 2 * half,), jnp.float32)]).reshape(1, LANES)
    return pl.pallas_call(
        _rope_table_kernel,
        out_shape=(jax.ShapeDtypeStruct((s, LANES), jnp.float32),) * 2,
        grid=(s // tm,),
        in_specs=[pl.BlockSpec((tm, 1), lambda i: (i, 0)),
                  pl.BlockSpec((1, LANES), lambda i: (0, 0))],
        out_specs=(pl.BlockSpec((tm, LANES), lambda i: (i, 0)),) * 2,
        compiler_params=_cparams(("parallel",)),
        name="rope_tables",
    )(pos_col, inv)


def _head_norm_rope(nope, rope128, gn, gr128, cos, sin, out_scale):
    half = ROPE_DIM // 2
    ss = jnp.sum(nope * nope, axis=-1, keepdims=True) + jnp.sum(rope128 * rope128, axis=-1, keepdims=True)
    scale = lax.rsqrt(ss * (1.0 / QK_DIM) + EPS) * out_scale
    rg = rope128 * gr128
    rot = pltpu.roll(rg, half, 1) - pltpu.roll(rg, LANES - half, 1)
    roped = rg * cos + rot * sin
    return jnp.concatenate([nope * gn * scale, (roped * scale)[:, :ROPE_DIM]], axis=-1)


def _kbuild_kernel(ckv_ref, w_ref, kr_ref, gn_ref, gr_ref, cos_ref, sin_ref, o_ref):
    kn = _dot(ckv_ref[...], w_ref[...])
    o_ref[...] = _bf16(_head_norm_rope(kn, kr_ref[...], gn_ref[...], gr_ref[...],
                                       cos_ref[...], sin_ref[...], 1.0))


def _qbuild_kernel(qa_ref, w_ref, gn_ref, gr_ref, cos_ref, sin_ref, o_ref):
    q = _dot(qa_ref[...], w_ref[...])
    o_ref[...] = _bf16(_head_norm_rope(q[:, :NOPE_DIM], q[:, NOPE_DIM:], gn_ref[...], gr_ref[...],
                                       cos_ref[...], sin_ref[...], QK_DIM ** -0.5))


def _norm_gain_split(g):
    gn = g[:NOPE_DIM].reshape(1, NOPE_DIM)
    gr = jnp.pad(g[NOPE_DIM:], (0, LANES - ROPE_DIM)).reshape(1, LANES)
    return gn, gr


def _kbuild(ckv, w_uk, kr_raw, k_norm_g, cos, sin, tm=1024):
    s, r = ckv.shape
    tm = min(tm, s)
    heads = w_uk.shape[1] // NOPE_DIM
    w = _bf16(w_uk).reshape(r, heads, NOPE_DIM).transpose(1, 0, 2)
    gn, gr = _norm_gain_split(k_norm_g)
    return pl.pallas_call(
        _kbuild_kernel,
        out_shape=jax.ShapeDtypeStruct((heads, s, QK_DIM), jnp.bfloat16),
        grid=(s // tm, heads),
        in_specs=[pl.BlockSpec((tm, r), lambda i, h: (i, 0)),
                  pl.BlockSpec((None, r, NOPE_DIM), lambda i, h: (h, 0, 0)),
                  pl.BlockSpec((tm, LANES), lambda i, h: (i, 0)),
                  pl.BlockSpec((1, NOPE_DIM), lambda i, h: (0, 0)),
                  pl.BlockSpec((1, LANES), lambda i, h: (0, 0)),
                  pl.BlockSpec((tm, LANES), lambda i, h: (i, 0)),
                  pl.BlockSpec((tm, LANES), lambda i, h: (i, 0))],
        out_specs=pl.BlockSpec((None, tm, QK_DIM), lambda i, h: (h, i, 0)),
        compiler_params=_cparams(("parallel", "arbitrary")),
        name="mla_k_build",
    )(ckv, w, kr_raw, gn, gr, cos, sin)


def _qbuild(qa, w_uq, q_norm_g, cos, sin, tm=1024):
    s, r = qa.shape
    tm = min(tm, s)
    heads = w_uq.shape[1] // QK_DIM
    w = _bf16(w_uq).reshape(r, heads, QK_DIM).transpose(1, 0, 2)
    w = jnp.pad(w, ((0, 0), (0, 0), (0, 2 * LANES - QK_DIM)))
    gn, gr = _norm_gain_split(q_norm_g)
    return pl.pallas_call(
        _qbuild_kernel,
        out_shape=jax.ShapeDtypeStruct((heads, s, QK_DIM), jnp.bfloat16),
        grid=(s // tm, heads),
        in_specs=[pl.BlockSpec((tm, r), lambda i, h: (i, 0)),
                  pl.BlockSpec((None, r, 2 * LANES), lambda i, h: (h, 0, 0)),
                  pl.BlockSpec((1, NOPE_DIM), lambda i, h: (0, 0)),
                  pl.BlockSpec((1, LANES), lambda i, h: (0, 0)),
                  pl.BlockSpec((tm, LANES), lambda i, h: (i, 0)),
                  pl.BlockSpec((tm, LANES), lambda i, h: (i, 0))],
        out_specs=pl.BlockSpec((None, tm, QK_DIM), lambda i, h: (h, i, 0)),
        compiler_params=_cparams(("parallel", "arbitrary")),
        name="mla_q_build",
    )(qa, w, gn, gr, cos, sin)


def _attn_kernel(nkv_ref, q_ref, k_ref, v_ref, qpos_ref, kpos_ref, o_ref, *, tk):
    i = pl.program_id(1)
    q = q_ref[...]
    tq = q.shape[0]
    qc = qpos_ref[...] >> CHUNK_SHIFT

    def step(j, carry):
        m, l, acc = carry
        off = pl.multiple_of(j * tk, tk)
        k = k_ref[pl.ds(off, tk), :]
        v = v_ref[pl.ds(off, tk), :]
        kc = kpos_ref[:, pl.ds(off, tk)] >> CHUNK_SHIFT
        s = lax.dot_general(q, k, (((1,), (1,)), ((), ())), preferred_element_type=jnp.float32)
        s = jnp.where(kc <= qc, s, NEG_BIG)
        m_new = jnp.maximum(m, jnp.max(s, axis=-1, keepdims=True))
        alpha = jnp.exp(m - m_new)
        p = jnp.exp(s - m_new)
        l = alpha * l + jnp.sum(p, axis=-1, keepdims=True)
        acc = alpha * acc + _dot(_bf16(p), v)
        return m_new, l, acc

    m0 = jnp.full((tq, 1), NEG_BIG, jnp.float32)
    l0 = jnp.zeros((tq, 1), jnp.float32)
    a0 = jnp.zeros((tq, V_DIM), jnp.float32)
    _, l, acc = lax.fori_loop(0, nkv_ref[i], step, (m0, l0, a0))
    o_ref[...] = _bf16(acc / l)


def _attention(q, k, v, pos_col, pos_row, tq=256, tk=512):
    heads, s, _ = q.shape
    nkv = _kv_tiles_needed(pos_row.reshape(-1), tq, tk)
    return pl.pallas_call(
        functools.partial(_attn_kernel, tk=tk),
        out_shape=jax.ShapeDtypeStruct((s, heads * V_DIM), jnp.bfloat16),
        grid_spec=pltpu.PrefetchScalarGridSpec(
            num_scalar_prefetch=1,
            grid=(heads, s // tq),
            in_specs=[pl.BlockSpec((None, tq, QK_DIM), lambda h, i, nkv: (h, i, 0)),
                      pl.BlockSpec((None, s, QK_DIM), lambda h, i, nkv: (h, 0, 0)),
                      pl.BlockSpec((s, V_DIM), lambda h, i, nkv: (0, h)),
                      pl.BlockSpec((tq, 1), lambda h, i, nkv: (i, 0)),
                      pl.BlockSpec((1, s), lambda h, i, nkv: (0, 0))],
            out_specs=pl.BlockSpec((tq, V_DIM), lambda h, i, nkv: (i, h))),
        compiler_params=_cparams(("parallel", "arbitrary")),
        name="mla_attention",
    )(nkv, q, k, v, pos_col, pos_row)


def _kv_tiles_needed(positions, tq, tk):
    cid = positions >> CHUNK_SHIFT
    qmax = jnp.max(cid.reshape(-1, tq), axis=1)
    kmin = jnp.min(cid.reshape(-1, tk), axis=1)
    vis = kmin[None, :] <= qmax[:, None]
    last = jnp.max(jnp.where(vis, jnp.arange(kmin.shape[0], dtype=jnp.int32)[None, :], -1), axis=1)
    return (last + 1).astype(jnp.int32)


def _shared_kv(x, cos, sin, kv_norm_g, w_dkv, kv_a_norm_g, w_kr, w_uk, w_uv, k_norm_g):
    hk = _prenorm(x, kv_norm_g)
    ckv = _mm(hk, _bf16(w_dkv), out_dtype=jnp.bfloat16, tm=1024, tn=w_dkv.shape[1],
              mode="rmsnorm", extra=kv_a_norm_g, name="mla_ckv")
    w_kr_pad = jnp.pad(_bf16(w_kr), ((0, 0), (0, LANES - ROPE_DIM)))
    kr_raw = _mm(hk, w_kr_pad, out_dtype=jnp.float32, tm=1024, tn=LANES, name="mla_krope")
    k = _kbuild(ckv, w_uk, kr_raw, k_norm_g, cos, sin)
    v = _mm(ckv, _bf16(w_uv), out_dtype=jnp.bfloat16, tm=1024, tn=1024, name="mla_v")
    return k, v


def _mla_layer(x, g_mix, k, v, cos, sin, pos_col, pos_row, w_dq, q_a_norm_g, w_uq, q_norm_g, w_o):
    h = _prenorm(x, g_mix)
    qa = _mm(h, _bf16(w_dq), out_dtype=jnp.bfloat16, tm=1024, tn=w_dq.shape[1],
             mode="rmsnorm", extra=q_a_norm_g, name="mla_qa")
    q = _qbuild(qa, w_uq, q_norm_g, cos, sin)
    o = _attention(q, k, v, pos_col, pos_row)
    return _mm(o, _bf16(w_o), out_dtype=jnp.float32, tm=1024, tn=1024, mode="residual", extra=x,
               name="mla_out_proj")


def kernel(x, positions, g_mix, g_ffn, ssm_w_in, ssm_conv_w, ssm_conv_b, ssm_dt_bias, ssm_a_log, ssm_d, ssm_norm_g, ssm_w_out, kv_norm_g, w_dkv, kv_a_norm_g, w_kr, w_uk, w_uv, k_norm_g, q_w_dq, q_a_norm_g, q_w_uq, q_norm_g, attn_w_o, moe_w_group, moe_w_expert, moe_w_gate, moe_w_up, moe_w_down):
    bsz, seq, d = x.shape
    assert bsz == 1
    depth = g_mix.shape[0]
    n_a = ssm_w_in.shape[0]
    xs = x.reshape(seq, d)
    pos = positions.reshape(seq).astype(jnp.int32)
    pos_col = pos.reshape(seq, 1)
    pos_row = pos.reshape(1, seq)
    cos, sin = _rope_tables(pos_col)
    k_shared = v_shared = None
    for layer in range(depth):
        if layer < n_a:
            i = layer
            xs = _mamba_layer(xs, g_mix[layer], ssm_w_in[i], ssm_conv_w[i], ssm_conv_b[i],
                              ssm_dt_bias[i], ssm_a_log[i], ssm_d[i], ssm_norm_g[i], ssm_w_out[i])
        else:
            j = layer - n_a
            xs = _mla_layer(xs, g_mix[layer], k_shared, v_shared, cos, sin, pos_col, pos_row,
                            q_w_dq[j], q_a_norm_g[j], q_w_uq[j], q_norm_g[j], attn_w_o[j])
        xs = _moe_layer(xs, g_ffn[layer], moe_w_group[layer], moe_w_expert[layer],
                        moe_w_gate, moe_w_up, moe_w_down, layer)
        if layer == n_a - 1:
            k_shared, v_shared = _shared_kv(xs, cos, sin, kv_norm_g, w_dkv, kv_a_norm_g, w_kr,
                                            w_uk, w_uv, k_norm_g)
    return xs.reshape(bsz, seq, d)
```

```python
import functools

import jax
import jax.numpy as jnp
from jax import lax
from jax.experimental import pallas as pl
from jax.experimental.pallas import tpu as pltpu

EPS = 1e-6
CHUNK = 64
CHUNK_SHIFT = 6

SSM_HEAD_DIM = 64
SSM_GROUPS = 8
SSM_STATE = 128
CONV_WIDTH = 4
SSD_L = 128
CONV_PAD = 8

NOPE_DIM = 128
ROPE_DIM = 64
QK_DIM = NOPE_DIM + ROPE_DIM
V_DIM = 128
ROPE_THETA = 10000.0

N_GROUPS = 4
EXPERTS_PER_GROUP = 8
N_EXPERTS = N_GROUPS * EXPERTS_PER_GROUP
TOP_K = 2
MOE_TM = 256

LANES = 128
VMEM_LIMIT_BYTES = 56 * 1024 * 1024
NEG_BIG = -1e30
LOG2E = 1.4426950408889634
Q_SCALE = QK_DIM ** -0.5 * LOG2E
HEADS_PER_STEP = 4
ATTN_GROUP_SHIFT = 2
ATTN_GROUP = 1 << ATTN_GROUP_SHIFT


def _cparams(sem):
    return pltpu.CompilerParams(dimension_semantics=sem, vmem_limit_bytes=VMEM_LIMIT_BYTES)


def _silu(v):
    return v * (1.0 / (1.0 + jnp.exp(-v)))


def _softplus(v):
    return jnp.maximum(v, 0.0) + jnp.log1p(jnp.exp(-jnp.abs(v)))


def _bf16(v):
    return v.astype(jnp.bfloat16)


def _dot(a, b):
    return jnp.dot(a, b, preferred_element_type=jnp.float32)


def _split3(v):
    hi = _bf16(v)
    r1 = v - hi.astype(jnp.float32)
    mid = _bf16(r1)
    lo = _bf16(r1 - mid.astype(jnp.float32))
    return hi, mid, lo


def _prenorm_kernel(x_ref, g_ref, o_ref):
    x = x_ref[...]
    y = x * lax.rsqrt(jnp.mean(x * x, axis=-1, keepdims=True) + EPS)
    o_ref[...] = _bf16(y * g_ref[...])


def _prenorm(x, g, tm=256):
    s, d = x.shape
    return pl.pallas_call(
        _prenorm_kernel,
        out_shape=jax.ShapeDtypeStruct((s, d), jnp.bfloat16),
        grid=(s // tm,),
        in_specs=[pl.BlockSpec((tm, d), lambda i: (i, 0)),
                  pl.BlockSpec((1, d), lambda i: (0, 0))],
        out_specs=pl.BlockSpec((tm, d), lambda i: (i, 0)),
        compiler_params=_cparams(("parallel",)),
        name="prenorm",
    )(x, g.reshape(1, d))


def _mm_kernel(*refs, nk, mode):
    a_ref, w_ref = refs[0], refs[1]
    if mode in ("plain", "transpose"):
        extra, o_ref, rest = None, refs[2], refs[3:]
    else:
        extra, o_ref, rest = refs[2], refs[3], refs[4:]

    def finish(acc):
        if mode == "transpose":
            acc = acc.T
        elif mode == "residual":
            acc = acc + extra[...]
        elif mode == "rmsnorm":
            acc = acc * lax.rsqrt(jnp.mean(acc * acc, axis=-1, keepdims=True) + EPS) * extra[...]
        o_ref[...] = acc.astype(o_ref.dtype)

    if nk == 1 and mode == "transpose":
        rest[0][...] = _dot(a_ref[...], w_ref[...])
        finish(rest[0][...])
        return
    if nk == 1:
        finish(_dot(a_ref[...], w_ref[...]))
        return

    acc_ref = rest[0]
    k = pl.program_id(2)

    @pl.when(k == 0)
    def _():
        acc_ref[...] = jnp.zeros_like(acc_ref)

    acc_ref[...] += _dot(a_ref[...], w_ref[...])

    @pl.when(k == nk - 1)
    def _():
        finish(acc_ref[...])


def _mm(a, w, *, out_dtype, tm, tn, tk=None, mode="plain", extra=None, name="mm"):
    m, kdim = a.shape
    n = w.shape[1]
    tm = min(tm, m)
    tk = kdim if tk is None else tk
    nk = kdim // tk
    in_specs = [pl.BlockSpec((tm, tk), lambda i, j, k: (i, k)),
                pl.BlockSpec((tk, tn), lambda i, j, k: (k, j))]
    args = [a, w]
    if mode == "residual":
        in_specs.append(pl.BlockSpec((tm, tn), lambda i, j, k: (i, j)))
        args.append(extra)
    elif mode == "rmsnorm":
        assert tn == n
        in_specs.append(pl.BlockSpec((1, tn), lambda i, j, k: (0, j)))
        args.append(extra.reshape(1, n))
    scratch = [pltpu.VMEM((tm, tn), jnp.float32)] if (nk > 1 or mode == "transpose") else []
    if mode == "transpose":
        out_shape = jax.ShapeDtypeStruct((n, m), out_dtype)
        out_spec = pl.BlockSpec((tn, tm), lambda i, j, k: (j, i))
    else:
        out_shape = jax.ShapeDtypeStruct((m, n), out_dtype)
        out_spec = pl.BlockSpec((tm, tn), lambda i, j, k: (i, j))
    return pl.pallas_call(
        functools.partial(_mm_kernel, nk=nk, mode=mode),
        out_shape=out_shape,
        grid=(m // tm, n // tn, nk),
        in_specs=in_specs,
        out_specs=out_spec,
        scratch_shapes=scratch,
        compiler_params=_cparams(("parallel", "parallel", "arbitrary")),
        name=name,
    )(*args)


def _pair_expand(cols, j, lane_lo):
    return jnp.where(lane_lo, cols[:, 2 * j:2 * j + 1], cols[:, 2 * j + 1:2 * j + 2])


def _ssd_kernel(z_ref, x_ref, b_ref, c_ref, cwx_ref, cwb_ref, cwc_ref, cbx_ref, cbb_ref, cbc_ref,
                dtc_ref, dtr_ref, biasc_ref, biasr_ref, alogc_ref, alogr_ref, dskip_ref, ng_ref,
                o_ref, xbuf, bbuf, cbuf, state):
    L = SSD_L
    c = pl.program_id(1)

    @pl.when(c == 0)
    def _():
        xbuf[0:CONV_PAD, :] = jnp.zeros((CONV_PAD, xbuf.shape[1]), jnp.float32)
        bbuf[0:CONV_PAD, :] = jnp.zeros((CONV_PAD, bbuf.shape[1]), jnp.float32)
        cbuf[0:CONV_PAD, :] = jnp.zeros((CONV_PAD, cbuf.shape[1]), jnp.float32)
        state[...] = jnp.zeros_like(state)

    def conv_silu(src_ref, buf, w_ref, bias_ref):
        buf[CONV_PAD:CONV_PAD + L, :] = src_ref[...].astype(jnp.float32)
        acc = bias_ref[...] + w_ref[CONV_WIDTH - 1:CONV_WIDTH, :] * buf[CONV_PAD:CONV_PAD + L, :]
        for k in range(CONV_WIDTH - 1):
            off = CONV_PAD - (CONV_WIDTH - 1) + k
            acc = acc + w_ref[k:k + 1, :] * buf[off:off + L, :]
        buf[0:CONV_PAD, :] = buf[L:L + CONV_PAD, :]
        return _silu(acc)

    xs = conv_silu(x_ref, xbuf, cwx_ref, cbx_ref)
    bs = conv_silu(b_ref, bbuf, cwb_ref, cbb_ref)
    cs = conv_silu(c_ref, cbuf, cwc_ref, cbc_ref)
    bs16 = _bf16(bs)
    cs16 = _bf16(cs)

    dtc = _softplus(dtc_ref[...] + biasc_ref[...])
    dtr = _softplus(dtr_ref[...] + biasr_ref[...])
    a_c = dtc * (-jnp.exp(alogc_ref[...]))
    a_r = dtr * (-jnp.exp(alogr_ref[...]))

    rid = lax.broadcasted_iota(jnp.int32, (L, L), 0)
    cid = lax.broadcasted_iota(jnp.int32, (L, L), 1)
    causal = rid >= cid
    tril = _bf16(jnp.where(causal, 1.0, 0.0))
    triu = _bf16(jnp.where(rid <= cid, 1.0, 0.0))
    acum_c = sum(_dot(tril, t) for t in _split3(a_c))
    acum_r = sum(_dot(t, triu) for t in _split3(a_r))

    total_c = acum_c[L - 1:L, :]
    e_c = jnp.exp(acum_c)
    w_c = dtc * jnp.exp(total_c - acum_c)
    etot = jnp.exp(total_c)

    cb = lax.dot_general(cs16, bs16, (((1,), (1,)), ((), ())),
                         preferred_element_type=jnp.float32)
    st = state[...]
    yoff = _dot(cs16, _bf16(st))

    lane = lax.broadcasted_iota(jnp.int32, (L, LANES), 1)
    lane_lo = lane < SSM_HEAD_DIM
    lane1_lo = lax.broadcasted_iota(jnp.int32, (1, LANES), 1) < SSM_HEAD_DIM

    z = z_ref[...].astype(jnp.float32)
    n_pairs = xs.shape[1] // LANES
    ys = []
    xws = []
    for j in range(n_pairs):
        xp = xs[:, j * LANES:(j + 1) * LANES]
        ms = []
        for hh in (2 * j, 2 * j + 1):
            seg = acum_c[:, hh:hh + 1] - acum_r[hh:hh + 1, :]
            dec = jnp.where(causal, jnp.exp(seg), 0.0)
            ms.append(_bf16(cb * dec * dtr[hh:hh + 1, :]))
        mpair = jnp.concatenate(ms, axis=1)
        rhs = jnp.concatenate([jnp.where(lane_lo, xp, 0.0), jnp.where(lane_lo, 0.0, xp)], axis=0)
        ydiag = _dot(mpair, _bf16(rhs))
        y = ydiag + yoff[:, j * LANES:(j + 1) * LANES] * _pair_expand(e_c, j, lane_lo)
        y = y + xp * dskip_ref[:, j * LANES:(j + 1) * LANES]
        ys.append(y)
        xws.append(_bf16(xp * _pair_expand(w_c, j, lane_lo)))
    y = jnp.concatenate(ys, axis=1)
    xw = jnp.concatenate(xws, axis=1)

    snew = lax.dot_general(bs16, xw, (((0,), (0,)), ((), ())),
                           preferred_element_type=jnp.float32)
    dec_row = jnp.concatenate([_pair_expand(etot, j, lane1_lo) for j in range(n_pairs)], axis=1)
    state[...] = st * dec_row + snew

    y = y * _silu(z)
    y = y * lax.rsqrt(jnp.mean(y * y, axis=-1, keepdims=True) + EPS)
    o_ref[...] = _bf16(y * ng_ref[...])


def _ssd(proj, dt_raw, conv_w, conv_b, dt_bias, a_log, d_skip, norm_g):
    s = proj.shape[0]
    g = SSM_GROUPS
    heads = dt_raw.shape[1]
    hg = heads // g
    gw = hg * SSM_HEAD_DIM
    d_inner = g * gw
    n = SSM_STATE
    L = SSD_L
    xb = d_inner // gw
    bb = 2 * d_inner // n
    cb_ = bb + g
    cwb = d_inner // n
    cwc = cwb + g

    dtc = dt_raw.reshape(s, g, hg).transpose(1, 0, 2)
    dtr = dt_raw.reshape(s, g, hg).transpose(1, 2, 0)
    cw = conv_w.reshape(CONV_WIDTH, -1)
    cbias = conv_b.reshape(1, -1)
    dsk = jnp.repeat(d_skip.reshape(g, 1, hg), SSM_HEAD_DIM, axis=2)

    def gspec(shape, fn):
        return pl.BlockSpec(shape, fn)

    in_specs = [
        gspec((L, gw), lambda gi, c: (c, gi)),
        gspec((L, gw), lambda gi, c: (c, xb + gi)),
        gspec((L, n), lambda gi, c: (c, bb + gi)),
        gspec((L, n), lambda gi, c: (c, cb_ + gi)),
        gspec((CONV_WIDTH, gw), lambda gi, c: (0, gi)),
        gspec((CONV_WIDTH, n), lambda gi, c: (0, cwb + gi)),
        gspec((CONV_WIDTH, n), lambda gi, c: (0, cwc + gi)),
        gspec((1, gw), lambda gi, c: (0, gi)),
        gspec((1, n), lambda gi, c: (0, cwb + gi)),
        gspec((1, n), lambda gi, c: (0, cwc + gi)),
        gspec((None, L, hg), lambda gi, c: (gi, c, 0)),
        gspec((None, hg, L), lambda gi, c: (gi, 0, c)),
        gspec((None, 1, hg), lambda gi, c: (gi, 0, 0)),
        gspec((None, hg, 1), lambda gi, c: (gi, 0, 0)),
        gspec((None, 1, hg), lambda gi, c: (gi, 0, 0)),
        gspec((None, hg, 1), lambda gi, c: (gi, 0, 0)),
        gspec((None, 1, gw), lambda gi, c: (gi, 0, 0)),
        gspec((1, gw), lambda gi, c: (0, gi)),
    ]
    return pl.pallas_call(
        _ssd_kernel,
        out_shape=jax.ShapeDtypeStruct((s, d_inner), jnp.bfloat16),
        grid=(g, s // L),
        in_specs=in_specs,
        out_specs=pl.BlockSpec((L, gw), lambda gi, c: (c, gi)),
        scratch_shapes=[pltpu.VMEM((CONV_PAD + L, gw), jnp.float32),
                        pltpu.VMEM((CONV_PAD + L, n), jnp.float32),
                        pltpu.VMEM((CONV_PAD + L, n), jnp.float32),
                        pltpu.VMEM((n, gw), jnp.float32)],
        compiler_params=_cparams(("parallel", "arbitrary")),
        name="ssd",
    )(proj, proj, proj, proj, cw, cw, cw, cbias, cbias, cbias, dtc, dtr,
      dt_bias.reshape(g, 1, hg), dt_bias.reshape(g, hg, 1),
      a_log.reshape(g, 1, hg), a_log.reshape(g, hg, 1), dsk, norm_g.reshape(1, d_inner))


def _mamba_layer(x, g_mix, w_in, conv_w, conv_b, dt_bias, a_log, d_skip, norm_g, w_out):
    d_inner = w_out.shape[0]
    n_main = 2 * d_inner + 2 * SSM_GROUPS * SSM_STATE
    h = _prenorm(x, g_mix)
    w16 = _bf16(w_in)
    proj = _mm(h, w16[:, :n_main], out_dtype=jnp.bfloat16, tm=1024, tn=1024, name="ssm_in_proj")
    dt_raw = _mm(h, w16[:, n_main:], out_dtype=jnp.float32, tm=1024, tn=w_in.shape[1] - n_main,
                 name="ssm_dt_proj")
    y = _ssd(proj, dt_raw, conv_w, conv_b, dt_bias, a_log, d_skip, norm_g)
    return _mm(y, _bf16(w_out), out_dtype=jnp.float32, tm=1024, tn=1024, tk=2048,
               mode="residual", extra=x, name="ssm_out_proj")


def _router_kernel(x_ref, g_ref, whi_ref, wlo_ref, oi_ref, of_ref, cnt_ref, carry):
    i = pl.program_id(0)
    tm = x_ref.shape[0]

    @pl.when(i == 0)
    def _():
        carry[...] = jnp.zeros_like(carry)

    x = x_ref[...]
    h = x * lax.rsqrt(jnp.mean(x * x, axis=-1, keepdims=True) + EPS) * g_ref[...]
    hhi = _bf16(h)
    hlo = _bf16(h - hhi.astype(jnp.float32))
    whi = whi_ref[...]
    logits = _dot(hhi, whi) + (_dot(hlo, whi) + _dot(hhi, wlo_ref[...]))

    lane = lax.broadcasted_iota(jnp.int32, (tm, LANES), 1)
    ninf = -jnp.inf
    gmask = lane < N_GROUPS
    gl = jnp.where(gmask, logits, ninf)
    gmax = jnp.max(gl, axis=-1, keepdims=True)
    gsel = jnp.min(jnp.where(gl == gmax, lane, LANES), axis=-1, keepdims=True)
    psel = 1.0 / jnp.sum(jnp.where(gmask, jnp.exp(logits - gmax), 0.0), axis=-1, keepdims=True)
    lo = N_GROUPS + EXPERTS_PER_GROUP * gsel
    emask = (lane >= lo) & (lane < lo + EXPERTS_PER_GROUP)
    el = jnp.where(emask, logits, ninf)
    v1 = jnp.max(el, axis=-1, keepdims=True)
    i1 = jnp.min(jnp.where(el == v1, lane, LANES), axis=-1, keepdims=True)
    el2 = jnp.where(lane == i1, ninf, el)
    v2 = jnp.max(el2, axis=-1, keepdims=True)
    i2 = jnp.min(jnp.where(el2 == v2, lane, LANES), axis=-1, keepdims=True)
    e21 = jnp.exp(v2 - v1)
    w1 = psel / (1.0 + e21)
    w2 = psel * e21 / (1.0 + e21)

    hit1 = lane == i1
    hit2 = lane == i2
    onehot = jnp.where(hit1 | hit2, 1.0, 0.0)
    rid = lax.broadcasted_iota(jnp.int32, (tm, tm), 0)
    cid = lax.broadcasted_iota(jnp.int32, (tm, tm), 1)
    strict = _bf16(jnp.where(rid > cid, 1.0, 0.0))
    before = _dot(strict, _bf16(onehot)) + carry[...]
    r1 = jnp.sum(jnp.where(hit1, before, 0.0), axis=-1, keepdims=True)
    r2 = jnp.sum(jnp.where(hit2, before, 0.0), axis=-1, keepdims=True)
    carry[...] += jnp.sum(onehot, axis=0, keepdims=True)

    e1 = i1 - N_GROUPS
    e2 = i2 - N_GROUPS
    oi = jnp.where(lane == 0, e1, jnp.where(lane == 1, e2, jnp.where(
        lane == 2, r1.astype(jnp.int32), jnp.where(lane == 3, r2.astype(jnp.int32), 0))))
    oi_ref[...] = oi
    of_ref[...] = jnp.where(lane == 0, w1, jnp.where(lane == 1, w2, 0.0))
    cnt_ref[...] = carry[...]


def _router(x, g_ffn, w_group, w_expert, tm=256):
    s, d = x.shape
    wr = jnp.concatenate([w_group, w_expert], axis=1)
    wr = jnp.pad(wr, ((0, 0), (0, LANES - wr.shape[1])))
    whi = _bf16(wr)
    wlo = _bf16(wr - whi.astype(jnp.float32))
    return pl.pallas_call(
        _router_kernel,
        out_shape=(jax.ShapeDtypeStruct((s, LANES), jnp.int32),
                   jax.ShapeDtypeStruct((s, LANES), jnp.float32),
                   jax.ShapeDtypeStruct((1, LANES), jnp.float32)),
        grid=(s // tm,),
        in_specs=[pl.BlockSpec((tm, d), lambda i: (i, 0)),
                  pl.BlockSpec((1, d), lambda i: (0, 0)),
                  pl.BlockSpec((d, LANES), lambda i: (0, 0)),
                  pl.BlockSpec((d, LANES), lambda i: (0, 0))],
        out_specs=(pl.BlockSpec((tm, LANES), lambda i: (i, 0)),
                   pl.BlockSpec((tm, LANES), lambda i: (i, 0)),
                   pl.BlockSpec((1, LANES), lambda i: (0, 0))),
        scratch_shapes=[pltpu.VMEM((1, LANES), jnp.float32)],
        compiler_params=_cparams(("arbitrary",)),
        name="moe_router",
    )(x, g_ffn.reshape(1, d), whi, wlo)


def _invert_kernel(pos_ref, src_ref):
    def clear(r, carry):
        src_ref[r] = 0
        return carry

    def place(a, carry):
        src_ref[pos_ref[a]] = lax.shift_right_logical(a, 1)
        return carry

    lax.fori_loop(0, src_ref.shape[0], clear, 0, unroll=8)
    lax.fori_loop(0, pos_ref.shape[0], place, 0, unroll=8)


def _invert(pos_flat, n_rows):
    assert TOP_K == 2
    return pl.pallas_call(
        _invert_kernel,
        out_shape=jax.ShapeDtypeStruct((n_rows,), jnp.int32),
        in_specs=[pl.BlockSpec(memory_space=pltpu.SMEM)],
        out_specs=pl.BlockSpec(memory_space=pltpu.SMEM),
        name="moe_invert",
    )(pos_flat)


def _expert_kernel(te_ref, tv_ref, src_ref, x_hbm, g_ref, wg_ref, wu_ref, wd_ref, ys_ref,
                   xbuf, wg16, wu16, wd16, sem):
    i = pl.program_id(0)
    nt = pl.num_programs(0)
    tm = xbuf.shape[1]

    def gather(tile, slot):
        def issue(r, carry):
            tok = src_ref[tile * tm + r]
            pltpu.make_async_copy(x_hbm.at[pl.ds(tok, 1)], xbuf.at[slot, pl.ds(r, 1)],
                                  sem.at[slot]).start()
            return carry
        lax.fori_loop(0, tm, issue, 0, unroll=8)

    @pl.when(i == 0)
    def _():
        gather(0, 0)

    nxt = jnp.minimum(i + 1, nt - 1)

    @pl.when((i + 1 < nt) & (tv_ref[nxt] > 0))
    def _():
        gather(i + 1, (i + 1) % 2)

    @pl.when((i == 0) | (te_ref[i] != te_ref[jnp.maximum(i - 1, 0)]))
    def _():
        wg16[...] = _bf16(wg_ref[...])
        wu16[...] = _bf16(wu_ref[...])
        wd16[...] = _bf16(wd_ref[...])

    @pl.when(tv_ref[i] > 0)
    def _():
        slot = i % 2
        pltpu.make_async_copy(x_hbm.at[pl.ds(0, tm)], xbuf.at[slot], sem.at[slot]).wait()
        x = xbuf[slot]
        h = _bf16(x * lax.rsqrt(jnp.mean(x * x, axis=-1, keepdims=True) + EPS) * g_ref[...])
        gate = _dot(h, wg16[...])
        up = _dot(h, wu16[...])
        hid = _bf16(_silu(gate) * up)
        ys_ref[...] = _dot(hid, wd16[...])

    @pl.when(tv_ref[i] == 0)
    def _():
        ys_ref[...] = jnp.zeros_like(ys_ref)


def _experts(x, src, g_ffn, w_gate, w_up, w_down, layer, tile_e, tile_v):
    d = x.shape[1]
    ff = w_gate.shape[-1]
    n_rows = src.shape[0]
    nt = n_rows // MOE_TM
    return pl.pallas_call(
        _expert_kernel,
        out_shape=jax.ShapeDtypeStruct((n_rows, d), jnp.float32),
        grid_spec=pltpu.PrefetchScalarGridSpec(
            num_scalar_prefetch=3,
            grid=(nt,),
            in_specs=[pl.BlockSpec(memory_space=pl.ANY),
                      pl.BlockSpec((1, d), lambda i, te, tv, src: (0, 0)),
                      pl.BlockSpec((None, None, d, ff), lambda i, te, tv, src: (layer, te[i], 0, 0)),
                      pl.BlockSpec((None, None, d, ff), lambda i, te, tv, src: (layer, te[i], 0, 0)),
                      pl.BlockSpec((None, None, ff, d), lambda i, te, tv, src: (layer, te[i], 0, 0))],
            out_specs=pl.BlockSpec((MOE_TM, d), lambda i, te, tv, src: (i, 0)),
            scratch_shapes=[pltpu.VMEM((2, MOE_TM, d), jnp.float32),
                            pltpu.VMEM((d, ff), jnp.bfloat16),
                            pltpu.VMEM((d, ff), jnp.bfloat16),
                            pltpu.VMEM((ff, d), jnp.bfloat16),
                            pltpu.SemaphoreType.DMA((2,))]),
        compiler_params=_cparams(("arbitrary",)),
        name="moe_experts",
    )(tile_e, tile_v, src, x, g_ffn.reshape(1, d), w_gate, w_up, w_down)


def _combine_kernel(pos_ref, x_ref, wt_ref, ys_hbm, o_ref, buf, sem, *, tc):
    i = pl.program_id(0)
    n = pl.num_programs(0)

    def gather(step, slot):
        def issue(t, carry):
            for k in range(TOP_K):
                p = pos_ref[TOP_K * (step * tc + t) + k]
                pltpu.make_async_copy(ys_hbm.at[pl.ds(p, 1)], buf.at[slot, k, pl.ds(t, 1)],
                                      sem.at[slot]).start()
            return carry
        lax.fori_loop(0, tc, issue, 0, unroll=8)

    @pl.when(i == 0)
    def _():
        gather(0, 0)

    @pl.when(i + 1 < n)
    def _():
        gather(i + 1, (i + 1) % 2)

    slot = i % 2
    for k in range(TOP_K):
        pltpu.make_async_copy(ys_hbm.at[pl.ds(0, tc)], buf.at[slot, k], sem.at[slot]).wait()
    wt = wt_ref[...]
    o_ref[...] = x_ref[...] + wt[:, 0:1] * buf[slot, 0] + wt[:, 1:2] * buf[slot, 1]


def _combine(x, wts, ys, pos_flat, tc=256):
    s, d = x.shape
    return pl.pallas_call(
        functools.partial(_combine_kernel, tc=tc),
        out_shape=jax.ShapeDtypeStruct((s, d), jnp.float32),
        grid_spec=pltpu.PrefetchScalarGridSpec(
            num_scalar_prefetch=1,
            grid=(s // tc,),
            in_specs=[pl.BlockSpec((tc, d), lambda i, pos: (i, 0)),
                      pl.BlockSpec((tc, LANES), lambda i, pos: (i, 0)),
                      pl.BlockSpec(memory_space=pl.ANY)],
            out_specs=pl.BlockSpec((tc, d), lambda i, pos: (i, 0)),
            scratch_shapes=[pltpu.VMEM((2, TOP_K, tc, d), jnp.float32),
                            pltpu.SemaphoreType.DMA((2,))]),
        compiler_params=_cparams(("arbitrary",)),
        name="moe_combine",
    )(pos_flat, x, wts, ys)


def _moe_layer(x, g_ffn, w_group, w_expert, w_gate, w_up, w_down, layer):
    s = x.shape[0]
    oi, wts, cnt = _router(x, g_ffn, w_group, w_expert)
    e = oi[:, 0:TOP_K]
    rank = oi[:, TOP_K:2 * TOP_K]
    counts = cnt[0, N_GROUPS:N_GROUPS + N_EXPERTS].astype(jnp.int32)
    tiles = (counts + MOE_TM - 1) // MOE_TM
    tile_end = jnp.cumsum(tiles)
    row_off = (tile_end - tiles) * MOE_TM
    pos = (row_off[e] + rank).reshape(-1).astype(jnp.int32)
    nt = s * TOP_K // MOE_TM + N_EXPERTS
    tidx = jnp.minimum(jnp.arange(nt, dtype=jnp.int32), tile_end[-1] - 1)
    tile_e = jnp.minimum(jnp.searchsorted(tile_end, tidx, side="right"), N_EXPERTS - 1)
    tile_v = (jnp.arange(nt, dtype=jnp.int32) < tile_end[-1]).astype(jnp.int32)
    src = _invert(pos, nt * MOE_TM)
    ys = _experts(x, src, g_ffn, w_gate, w_up, w_down, layer, tile_e.astype(jnp.int32), tile_v)
    return _combine(x, wts, ys, pos)


def _rope_table_kernel(pos_ref, inv_ref, cos_ref, sin_ref):
    ang = pos_ref[...].astype(jnp.float32) * inv_ref[...]
    cos_ref[...] = jnp.cos(ang)
    sin_ref[...] = jnp.sin(ang)


def _rope_tables(pos_col, tm=512):
    s = pos_col.shape[0]
    half = ROPE_DIM // 2
    inv = jnp.power(ROPE_THETA, -jnp.arange(0, ROPE_DIM, 2, dtype=jnp.float32) / ROPE_DIM)
    inv = jnp.concatenate([inv, inv, jnp.zeros((LANES - 2 * half,), jnp.float32)]).reshape(1, LANES)
    return pl.pallas_call(
        _rope_table_kernel,
        out_shape=(jax.ShapeDtypeStruct((s, LANES), jnp.float32),) * 2,
        grid=(s // tm,),
        in_specs=[pl.BlockSpec((tm, 1), lambda i: (i, 0)),
                  pl.BlockSpec((1, LANES), lambda i: (0, 0))],
        out_specs=(pl.BlockSpec((tm, LANES), lambda i: (i, 0)),) * 2,
        compiler_params=_cparams(("parallel",)),
        name="rope_tables",
    )(pos_col, inv)


def _rot_half_matrix():
    half = ROPE_DIM // 2
    r = jnp.arange(LANES)[:, None]
    c = jnp.arange(LANES)[None, :]
    plus = (c >= half) & (c < ROPE_DIM) & (r == c - half)
    minus = (c < half) & (r == c + half)
    return _bf16(plus.astype(jnp.float32) - minus.astype(jnp.float32))


def _head_norm_rope(o_ref, h, nope, rope128, gn, gr128, cos, sin, rot_m, out_scale):
    ones = jnp.ones((LANES, LANES), jnp.bfloat16)
    ss = _dot(_bf16(nope * nope), ones) + _dot(_bf16(rope128 * rope128), ones)
    scale = lax.rsqrt(ss * (1.0 / QK_DIM) + EPS) * out_scale
    rg = rope128 * gr128
    roped = rg * cos + _dot(_bf16(rg), rot_m) * sin
    o_ref[h, :, 0:NOPE_DIM] = _bf16(nope * gn * scale)
    o_ref[h, :, NOPE_DIM:QK_DIM] = _bf16((roped * scale)[:, :ROPE_DIM])


def _kbuild_kernel(ckv_ref, w_ref, kr_ref, gn_ref, gr_ref, cos_ref, sin_ref, rot_ref, o_ref):
    kns = [_dot(ckv_ref[...], w_ref[h]) for h in range(w_ref.shape[0])]
    for h, kn in enumerate(kns):
        _head_norm_rope(o_ref, h, kn, kr_ref[...], gn_ref[...], gr_ref[...], cos_ref[...], sin_ref[...],
                        rot_ref[...], 1.0)


def _qbuild_kernel(qa_ref, w_ref, gn_ref, gr_ref, cos_ref, sin_ref, rot_ref, o_ref):
    qs = [_dot(qa_ref[...], w_ref[h]) for h in range(w_ref.shape[0])]
    for h, q in enumerate(qs):
        _head_norm_rope(o_ref, h, q[:, :NOPE_DIM], q[:, NOPE_DIM:], gn_ref[...], gr_ref[...],
                        cos_ref[...], sin_ref[...], rot_ref[...], Q_SCALE)


def _norm_gain_split(g):
    gn = g[:NOPE_DIM].reshape(1, NOPE_DIM)
    gr = jnp.pad(g[NOPE_DIM:], (0, LANES - ROPE_DIM)).reshape(1, LANES)
    return gn, gr


def _kbuild(ckv, w_uk, kr_raw, k_norm_g, cos, sin, tm=512, hb=HEADS_PER_STEP):
    s, r = ckv.shape
    tm = min(tm, s)
    heads = w_uk.shape[1] // NOPE_DIM
    w = _bf16(w_uk).reshape(r, heads, NOPE_DIM).transpose(1, 0, 2)
    gn, gr = _norm_gain_split(k_norm_g)
    return pl.pallas_call(
        _kbuild_kernel,
        out_shape=jax.ShapeDtypeStruct((heads, s, QK_DIM), jnp.bfloat16),
        grid=(s // tm, heads // hb),
        in_specs=[pl.BlockSpec((tm, r), lambda i, h: (i, 0)),
                  pl.BlockSpec((hb, r, NOPE_DIM), lambda i, h: (h, 0, 0)),
                  pl.BlockSpec((tm, LANES), lambda i, h: (i, 0)),
                  pl.BlockSpec((1, NOPE_DIM), lambda i, h: (0, 0)),
                  pl.BlockSpec((1, LANES), lambda i, h: (0, 0)),
                  pl.BlockSpec((tm, LANES), lambda i, h: (i, 0)),
                  pl.BlockSpec((tm, LANES), lambda i, h: (i, 0)),
                  pl.BlockSpec((LANES, LANES), lambda i, h: (0, 0))],
        out_specs=pl.BlockSpec((hb, tm, QK_DIM), lambda i, h: (h, i, 0)),
        compiler_params=_cparams(("parallel", "arbitrary")),
        name="mla_k_build",
    )(ckv, w, kr_raw, gn, gr, cos, sin, _rot_half_matrix())


def _qbuild(qa, w_uq, q_norm_g, cos, sin, tm=512, hb=HEADS_PER_STEP):
    s, r = qa.shape
    tm = min(tm, s)
    heads = w_uq.shape[1] // QK_DIM
    w = _bf16(w_uq).reshape(r, heads, QK_DIM).transpose(1, 0, 2)
    w = jnp.pad(w, ((0, 0), (0, 0), (0, 2 * LANES - QK_DIM)))
    gn, gr = _norm_gain_split(q_norm_g)
    return pl.pallas_call(
        _qbuild_kernel,
        out_shape=jax.ShapeDtypeStruct((heads, s, QK_DIM), jnp.bfloat16),
        grid=(s // tm, heads // hb),
        in_specs=[pl.BlockSpec((tm, r), lambda i, h: (i, 0)),
                  pl.BlockSpec((hb, r, 2 * LANES), lambda i, h: (h, 0, 0)),
                  pl.BlockSpec((1, NOPE_DIM), lambda i, h: (0, 0)),
                  pl.BlockSpec((1, LANES), lambda i, h: (0, 0)),
                  pl.BlockSpec((tm, LANES), lambda i, h: (i, 0)),
                  pl.BlockSpec((tm, LANES), lambda i, h: (i, 0)),
                  pl.BlockSpec((LANES, LANES), lambda i, h: (0, 0))],
        out_specs=pl.BlockSpec((hb, tm, QK_DIM), lambda i, h: (h, i, 0)),
        compiler_params=_cparams(("parallel", "arbitrary")),
        name="mla_q_build",
    )(qa, w, gn, gr, cos, sin, _rot_half_matrix())


def _reduce_rows(x, pair_op, final_op):
    rows = x.shape[0]
    while rows > 8:
        rows //= 2
        x = pair_op(x[:rows], x[rows:])
    return final_op(x, axis=0, keepdims=True)


def _attn_kernel(nfull_ref, nkv_ref, q_ref, k_ref, vt_ref, kpos_ref, qpos_ref, o_ref, *, tk, n_sub):
    i = pl.program_id(1)
    sub = q_ref.shape[0] // n_sub
    qs = [q_ref[h * sub:(h + 1) * sub, :] for h in range(n_sub)]
    qcs = [qpos_ref[:, h * sub:(h + 1) * sub] >> CHUNK_SHIFT for h in range(n_sub)]

    def tile_stats(js, masked):
        offs = [pl.multiple_of(j * tk, tk) for j in js]
        chains = [(t, h) for t in range(len(js)) for h in range(n_sub)]
        scores = {}
        for t, h in chains:
            k = k_ref[pl.ds(offs[t], tk), :]
            s = lax.dot_general(k, qs[h], (((1,), (1,)), ((), ())),
                                preferred_element_type=jnp.float32)
            if masked:
                kc = kpos_ref[pl.ds(offs[t], tk), :] >> CHUNK_SHIFT
                s = jnp.where(kc <= qcs[h], s, NEG_BIG)
            scores[t, h] = s
        soft = {}
        for t, h in chains:
            m_t = _reduce_rows(scores[t, h], jnp.maximum, jnp.max)
            p = jnp.exp2(scores[t, h] - m_t)
            soft[t, h] = (m_t, _reduce_rows(p, jnp.add, jnp.sum), _bf16(p))
        out = [[None] * n_sub for _ in js]
        for t, h in chains:
            vt = vt_ref[:, pl.ds(offs[t], tk)]
            out[t][h] = (soft[t, h][0], soft[t, h][1], _dot(vt, soft[t, h][2]))
        return out

    def merge(carry, tiles):
        out = []
        for h in range(n_sub):
            m, l, acc = carry[h]
            m_new = m
            for t in tiles:
                m_new = jnp.maximum(m_new, t[h][0])
            a = jnp.exp2(m - m_new)
            l, acc = l * a, acc * a
            for t in tiles:
                b = jnp.exp2(t[h][0] - m_new)
                l, acc = l + t[h][1] * b, acc + t[h][2] * b
            out.append((m_new, l, acc))
        return tuple(out)

    def group_step(jj, carry):
        return merge(carry, tile_stats([ATTN_GROUP * jj + u for u in range(ATTN_GROUP)], False))

    def single_step(j, carry):
        return merge(carry, tile_stats([j], False))

    def masked_step(j, carry):
        return merge(carry, tile_stats([j], True))

    init = tuple((jnp.full((1, sub), NEG_BIG, jnp.float32), jnp.zeros((1, sub), jnp.float32),
                  jnp.zeros((V_DIM, sub), jnp.float32)) for _ in range(n_sub))
    n_groups = lax.shift_right_logical(nfull_ref[i], ATTN_GROUP_SHIFT)
    carry = lax.fori_loop(0, n_groups, group_step, init)
    carry = lax.fori_loop(ATTN_GROUP * n_groups, nfull_ref[i], single_step, carry)
    carry = lax.fori_loop(nfull_ref[i], nkv_ref[i], masked_step, carry)
    for h in range(n_sub):
        _, l, acc = carry[h]
        o_ref[h * sub:(h + 1) * sub, :] = _bf16((acc / l).T)


def _attention(q, k, vt, pos_col, pos_row, tq=512, tk=512, n_sub=2):
    heads, s, _ = q.shape
    tq = min(tq, s)
    tk = min(tk, s)
    nfull, nkv = _kv_tile_ranges(pos_row.reshape(-1), tq, tk)
    return pl.pallas_call(
        functools.partial(_attn_kernel, tk=tk, n_sub=n_sub),
        out_shape=jax.ShapeDtypeStruct((s, heads * V_DIM), jnp.bfloat16),
        grid_spec=pltpu.PrefetchScalarGridSpec(
            num_scalar_prefetch=2,
            grid=(heads, s // tq),
            in_specs=[pl.BlockSpec((None, tq, QK_DIM), lambda h, i, nf, nk: (h, i, 0)),
                      pl.BlockSpec((None, s, QK_DIM), lambda h, i, nf, nk: (h, 0, 0)),
                      pl.BlockSpec((V_DIM, s), lambda h, i, nf, nk: (h, 0)),
                      pl.BlockSpec((s, 1), lambda h, i, nf, nk: (0, 0)),
                      pl.BlockSpec((1, tq), lambda h, i, nf, nk: (0, i))],
            out_specs=pl.BlockSpec((tq, V_DIM), lambda h, i, nf, nk: (i, h))),
        compiler_params=_cparams(("parallel", "arbitrary")),
        name="mla_attention",
    )(nfull, nkv, q, k, vt, pos_col, pos_row)


def _kv_tile_ranges(positions, tq, tk):
    cid = positions >> CHUNK_SHIFT
    qc = cid.reshape(-1, tq)
    kc = cid.reshape(-1, tk)
    qmin, qmax = jnp.min(qc, axis=1), jnp.max(qc, axis=1)
    kmin, kmax = jnp.min(kc, axis=1), jnp.max(kc, axis=1)
    vis = kmin[None, :] <= qmax[:, None]
    last = jnp.max(jnp.where(vis, jnp.arange(kmin.shape[0], dtype=jnp.int32)[None, :], -1), axis=1)
    full = (kmax[None, :] <= qmin[:, None]).astype(jnp.int32)
    nfull = jnp.sum(jnp.cumprod(full, axis=1), axis=1)
    return nfull.astype(jnp.int32), (last + 1).astype(jnp.int32)


def _shared_kv(x, cos, sin, kv_norm_g, w_dkv, kv_a_norm_g, w_kr, w_uk, w_uv, k_norm_g):
    hk = _prenorm(x, kv_norm_g)
    ckv = _mm(hk, _bf16(w_dkv), out_dtype=jnp.bfloat16, tm=1024, tn=w_dkv.shape[1],
              mode="rmsnorm", extra=kv_a_norm_g, name="mla_ckv")
    w_kr_pad = jnp.pad(_bf16(w_kr), ((0, 0), (0, LANES - ROPE_DIM)))
    kr_raw = _mm(hk, w_kr_pad, out_dtype=jnp.float32, tm=1024, tn=LANES, name="mla_krope")
    k = _kbuild(ckv, w_uk, kr_raw, k_norm_g, cos, sin)
    vt = _mm(ckv, _bf16(w_uv), out_dtype=jnp.bfloat16, tm=1024, tn=1024, mode="transpose",
             name="mla_v")
    return k, vt


def _mla_layer(x, g_mix, k, v, cos, sin, pos_col, pos_row, w_dq, q_a_norm_g, w_uq, q_norm_g, w_o):
    h = _prenorm(x, g_mix)
    qa = _mm(h, _bf16(w_dq), out_dtype=jnp.bfloat16, tm=1024, tn=w_dq.shape[1],
             mode="rmsnorm", extra=q_a_norm_g, name="mla_qa")
    q = _qbuild(qa, w_uq, q_norm_g, cos, sin)
    o = _attention(q, k, v, pos_col, pos_row)
    return _mm(o, _bf16(w_o), out_dtype=jnp.float32, tm=1024, tn=1024, mode="residual", extra=x,
               name="mla_out_proj")


def kernel(x, positions, g_mix, g_ffn, ssm_w_in, ssm_conv_w, ssm_conv_b, ssm_dt_bias, ssm_a_log, ssm_d, ssm_norm_g, ssm_w_out, kv_norm_g, w_dkv, kv_a_norm_g, w_kr, w_uk, w_uv, k_norm_g, q_w_dq, q_a_norm_g, q_w_uq, q_norm_g, attn_w_o, moe_w_group, moe_w_expert, moe_w_gate, moe_w_up, moe_w_down):
    bsz, seq, d = x.shape
    assert bsz == 1
    depth = g_mix.shape[0]
    n_a = ssm_w_in.shape[0]
    xs = x.reshape(seq, d)
    pos = positions.reshape(seq).astype(jnp.int32)
    pos_col = pos.reshape(seq, 1)
    pos_row = pos.reshape(1, seq)
    cos, sin = _rope_tables(pos_col)
    k_shared = v_shared = None
    for layer in range(depth):
        if layer < n_a:
            i = layer
            xs = _mamba_layer(xs, g_mix[layer], ssm_w_in[i], ssm_conv_w[i], ssm_conv_b[i],
                              ssm_dt_bias[i], ssm_a_log[i], ssm_d[i], ssm_norm_g[i], ssm_w_out[i])
        else:
            j = layer - n_a
            xs = _mla_layer(xs, g_mix[layer], k_shared, v_shared, cos, sin, pos_col, pos_row,
                            q_w_dq[j], q_a_norm_g[j], q_w_uq[j], q_norm_g[j], attn_w_o[j])
        xs = _moe_layer(xs, g_ffn[layer], moe_w_group[layer], moe_w_expert[layer],
                        moe_w_gate, moe_w_up, moe_w_down, layer)
        if layer == n_a - 1:
            k_shared, v_shared = _shared_kv(xs, cos, sin, kv_norm_g, w_dkv, kv_a_norm_g, w_kr,
                                            w_uk, w_uv, k_norm_g)
    return xs.reshape(bsz, seq, d)
```

```python
import functools

import jax
import jax.numpy as jnp
from jax import lax
from jax.experimental import pallas as pl
from jax.experimental.pallas import tpu as pltpu

EPS = 1e-6
CHUNK = 64
CHUNK_SHIFT = 6

SSM_HEAD_DIM = 64
SSM_GROUPS = 8
SSM_STATE = 128
CONV_WIDTH = 4
SSD_L = 128
CONV_PAD = 8

NOPE_DIM = 128
ROPE_DIM = 64
QK_DIM = NOPE_DIM + ROPE_DIM
V_DIM = 128
ROPE_THETA = 10000.0

N_GROUPS = 4
EXPERTS_PER_GROUP = 8
N_EXPERTS = N_GROUPS * EXPERTS_PER_GROUP
TOP_K = 2
MOE_TM = 256

LANES = 128
VMEM_LIMIT_BYTES = 56 * 1024 * 1024
NEG_BIG = -1e30
LOG2E = 1.4426950408889634
Q_SCALE = QK_DIM ** -0.5 * LOG2E
HEADS_PER_STEP = 4
ATTN_GROUP_SHIFT = 2


def _cparams(sem):
    return pltpu.CompilerParams(dimension_semantics=sem, vmem_limit_bytes=VMEM_LIMIT_BYTES)


def _silu(v):
    hv = 0.5 * v
    return hv + hv * jnp.tanh(hv)


def _pack_bf16_pairs(v):
    c = v.shape[1] // 2
    bits = lax.bitcast_convert_type(_bf16(v).astype(jnp.float32), jnp.uint32)
    return bits[:, :c] | (bits[:, c:] >> 16)


def _unpack_bf16_pairs(u):
    hi = lax.bitcast_convert_type(u & jnp.uint32(0xFFFF0000), jnp.float32)
    lo = lax.bitcast_convert_type(u << 16, jnp.float32)
    return hi, lo


def _softplus(v):
    return jnp.maximum(v, 0.0) + jnp.log1p(jnp.exp(-jnp.abs(v)))


def _bf16(v):
    return v.astype(jnp.bfloat16)


def _dot(a, b):
    return jnp.dot(a, b, preferred_element_type=jnp.float32)


def _split3(v):
    hi = _bf16(v)
    r1 = v - hi.astype(jnp.float32)
    mid = _bf16(r1)
    lo = _bf16(r1 - mid.astype(jnp.float32))
    return hi, mid, lo


def _prenorm_kernel(x_ref, g_ref, o_ref):
    x = x_ref[...]
    y = x * lax.rsqrt(jnp.mean(x * x, axis=-1, keepdims=True) + EPS)
    o_ref[...] = _bf16(y * g_ref[...])


def _prenorm(x, g, tm=256):
    s, d = x.shape
    return pl.pallas_call(
        _prenorm_kernel,
        out_shape=jax.ShapeDtypeStruct((s, d), jnp.bfloat16),
        grid=(s // tm,),
        in_specs=[pl.BlockSpec((tm, d), lambda i: (i, 0)),
                  pl.BlockSpec((1, d), lambda i: (0, 0))],
        out_specs=pl.BlockSpec((tm, d), lambda i: (i, 0)),
        compiler_params=_cparams(("parallel",)),
        name="prenorm",
    )(x, g.reshape(1, d))


def _mm_kernel(*refs, nk, mode):
    a_ref, w_ref = refs[0], refs[1]
    if mode in ("plain", "transpose"):
        extra, o_ref, rest = None, refs[2], refs[3:]
    else:
        extra, o_ref, rest = refs[2], refs[3], refs[4:]

    def finish(acc):
        if mode == "transpose":
            acc = acc.T
        elif mode == "residual":
            acc = acc + extra[...]
        elif mode == "rmsnorm":
            acc = acc * lax.rsqrt(jnp.mean(acc * acc, axis=-1, keepdims=True) + EPS) * extra[...]
        o_ref[...] = acc.astype(o_ref.dtype)

    def product():
        return _dot(a_ref[...], _bf16(w_ref[...]))

    if nk == 1 and mode == "transpose":
        rest[0][...] = product()
        finish(rest[0][...])
        return
    if nk == 1:
        finish(product())
        return

    acc_ref = rest[0]
    k = pl.program_id(2)

    @pl.when(k == 0)
    def _():
        acc_ref[...] = jnp.zeros_like(acc_ref)

    acc_ref[...] += product()

    @pl.when(k == nk - 1)
    def _():
        finish(acc_ref[...])


def _mm(a, w, *, out_dtype, tm, tn, tk=None, mode="plain", extra=None, name="mm", layer=None,
        col0=0, n=None):
    m, kdim = a.shape
    n = w.shape[-1] - col0 if n is None else n
    tm = min(tm, m)
    tk = kdim if tk is None else tk
    nk = kdim // tk
    cb0 = col0 // tn
    assert col0 % tn == 0 and n % tn == 0 and kdim % tk == 0
    if layer is None:
        w_spec = pl.BlockSpec((tk, tn), lambda i, j, k: (k, j + cb0))
    else:
        w_spec = pl.BlockSpec((None, tk, tn), lambda i, j, k: (layer, k, j + cb0))
    in_specs = [pl.BlockSpec((tm, tk), lambda i, j, k: (i, k)), w_spec]
    args = [a, w]
    if mode == "residual":
        in_specs.append(pl.BlockSpec((tm, tn), lambda i, j, k: (i, j)))
        args.append(extra)
    elif mode == "rmsnorm":
        assert tn == n
        in_specs.append(pl.BlockSpec((1, tn), lambda i, j, k: (0, j)))
        args.append(extra.reshape(1, n))
    scratch = [pltpu.VMEM((tm, tn), jnp.float32)] if (nk > 1 or mode == "transpose") else []
    if mode == "transpose":
        out_shape = jax.ShapeDtypeStruct((n, m), out_dtype)
        out_spec = pl.BlockSpec((tn, tm), lambda i, j, k: (j, i))
    else:
        out_shape = jax.ShapeDtypeStruct((m, n), out_dtype)
        out_spec = pl.BlockSpec((tm, tn), lambda i, j, k: (i, j))
    return pl.pallas_call(
        functools.partial(_mm_kernel, nk=nk, mode=mode),
        out_shape=out_shape,
        grid=(m // tm, n // tn, nk),
        in_specs=in_specs,
        out_specs=out_spec,
        scratch_shapes=scratch,
        compiler_params=_cparams(("parallel", "parallel", "arbitrary")),
        name=name,
    )(*args)


def _pair_expand(cols, j, lane_lo):
    return jnp.where(lane_lo, cols[:, 2 * j:2 * j + 1], cols[:, 2 * j + 1:2 * j + 2])


def _ssd_kernel(z_ref, x_ref, b_ref, c_ref, cwx_ref, cwb_ref, cwc_ref, cbx_ref, cbb_ref, cbc_ref,
                dtc_ref, dtr_ref, biasc_ref, biasr_ref, alogc_ref, alogr_ref, dskip_ref, ng_ref,
                o_ref, xbuf, bbuf, cbuf, state):
    L = SSD_L
    c = pl.program_id(1)

    @pl.when(c == 0)
    def _():
        xbuf[0:CONV_PAD, :] = jnp.zeros((CONV_PAD, xbuf.shape[1]), jnp.float32)
        bbuf[0:CONV_PAD, :] = jnp.zeros((CONV_PAD, bbuf.shape[1]), jnp.float32)
        cbuf[0:CONV_PAD, :] = jnp.zeros((CONV_PAD, cbuf.shape[1]), jnp.float32)
        state[...] = jnp.zeros_like(state)

    def conv_silu(src_ref, buf, w_ref, bias_ref):
        buf[CONV_PAD:CONV_PAD + L, :] = src_ref[...].astype(jnp.float32)
        acc = bias_ref[...] + w_ref[CONV_WIDTH - 1:CONV_WIDTH, :] * buf[CONV_PAD:CONV_PAD + L, :]
        for k in range(CONV_WIDTH - 1):
            off = CONV_PAD - (CONV_WIDTH - 1) + k
            acc = acc + w_ref[k:k + 1, :] * buf[off:off + L, :]
        buf[0:CONV_PAD, :] = buf[L:L + CONV_PAD, :]
        return _silu(acc)

    xs = conv_silu(x_ref, xbuf, cwx_ref, cbx_ref)
    bs = conv_silu(b_ref, bbuf, cwb_ref, cbb_ref)
    cs = conv_silu(c_ref, cbuf, cwc_ref, cbc_ref)
    bs16 = _bf16(bs)
    cs16 = _bf16(cs)

    dtc = _softplus(dtc_ref[...] + biasc_ref[...])
    dtr = _softplus(dtr_ref[...] + biasr_ref[...])
    a_c = dtc * (-jnp.exp(alogc_ref[...]))
    a_r = dtr * (-jnp.exp(alogr_ref[...]))

    rid = lax.broadcasted_iota(jnp.int32, (L, L), 0)
    cid = lax.broadcasted_iota(jnp.int32, (L, L), 1)
    causal = rid >= cid
    tril = _bf16(jnp.where(causal, 1.0, 0.0))
    triu = _bf16(jnp.where(rid <= cid, 1.0, 0.0))
    acum_c = sum(_dot(tril, t) for t in _split3(a_c))
    acum_r = sum(_dot(t, triu) for t in _split3(a_r))

    total_c = acum_c[L - 1:L, :]
    e_c = jnp.exp(acum_c)
    w_c = dtc * jnp.exp(total_c - acum_c)
    etot = jnp.exp(total_c)

    cb = lax.dot_general(cs16, bs16, (((1,), (1,)), ((), ())),
                         preferred_element_type=jnp.float32)
    st = state[...]
    yoff = _dot(cs16, _bf16(st))

    lane = lax.broadcasted_iota(jnp.int32, (L, LANES), 1)
    lane_lo = lane < SSM_HEAD_DIM
    lane1_lo = lax.broadcasted_iota(jnp.int32, (1, LANES), 1) < SSM_HEAD_DIM

    z = z_ref[...].astype(jnp.float32)
    n_pairs = xs.shape[1] // LANES
    ys = []
    xws = []
    for j in range(n_pairs):
        xp = xs[:, j * LANES:(j + 1) * LANES]
        ms = []
        for hh in (2 * j, 2 * j + 1):
            seg = acum_c[:, hh:hh + 1] - acum_r[hh:hh + 1, :]
            dec = jnp.where(causal, jnp.exp(seg), 0.0)
            ms.append(_bf16(cb * dec * dtr[hh:hh + 1, :]))
        mpair = jnp.concatenate(ms, axis=1)
        rhs = jnp.concatenate([jnp.where(lane_lo, xp, 0.0), jnp.where(lane_lo, 0.0, xp)], axis=0)
        ydiag = _dot(mpair, _bf16(rhs))
        y = ydiag + yoff[:, j * LANES:(j + 1) * LANES] * _pair_expand(e_c, j, lane_lo)
        y = y + xp * dskip_ref[:, j * LANES:(j + 1) * LANES]
        ys.append(y)
        xws.append(_bf16(xp * _pair_expand(w_c, j, lane_lo)))
    y = jnp.concatenate(ys, axis=1)
    xw = jnp.concatenate(xws, axis=1)

    snew = lax.dot_general(bs16, xw, (((0,), (0,)), ((), ())),
                           preferred_element_type=jnp.float32)
    dec_row = jnp.concatenate([_pair_expand(etot, j, lane1_lo) for j in range(n_pairs)], axis=1)
    state[...] = st * dec_row + snew

    y = y * _silu(z)
    y = y * lax.rsqrt(jnp.mean(y * y, axis=-1, keepdims=True) + EPS)
    o_ref[...] = _bf16(y * ng_ref[...])


def _ssd(proj, dt_raw, conv_w, conv_b, dt_bias, a_log, d_skip, norm_g):
    s = proj.shape[0]
    g = SSM_GROUPS
    heads = dt_raw.shape[1]
    hg = heads // g
    gw = hg * SSM_HEAD_DIM
    d_inner = g * gw
    n = SSM_STATE
    L = SSD_L
    xb = d_inner // gw
    bb = 2 * d_inner // n
    cb_ = bb + g
    cwb = d_inner // n
    cwc = cwb + g

    dtc = dt_raw.reshape(s, g, hg).transpose(1, 0, 2)
    dtr = dt_raw.reshape(s, g, hg).transpose(1, 2, 0)
    cw = conv_w.reshape(CONV_WIDTH, -1)
    cbias = conv_b.reshape(1, -1)
    dsk = jnp.repeat(d_skip.reshape(g, 1, hg), SSM_HEAD_DIM, axis=2)

    def gspec(shape, fn):
        return pl.BlockSpec(shape, fn)

    in_specs = [
        gspec((L, gw), lambda gi, c: (c, gi)),
        gspec((L, gw), lambda gi, c: (c, xb + gi)),
        gspec((L, n), lambda gi, c: (c, bb + gi)),
        gspec((L, n), lambda gi, c: (c, cb_ + gi)),
        gspec((CONV_WIDTH, gw), lambda gi, c: (0, gi)),
        gspec((CONV_WIDTH, n), lambda gi, c: (0, cwb + gi)),
        gspec((CONV_WIDTH, n), lambda gi, c: (0, cwc + gi)),
        gspec((1, gw), lambda gi, c: (0, gi)),
        gspec((1, n), lambda gi, c: (0, cwb + gi)),
        gspec((1, n), lambda gi, c: (0, cwc + gi)),
        gspec((None, L, hg), lambda gi, c: (gi, c, 0)),
        gspec((None, hg, L), lambda gi, c: (gi, 0, c)),
        gspec((None, 1, hg), lambda gi, c: (gi, 0, 0)),
        gspec((None, hg, 1), lambda gi, c: (gi, 0, 0)),
        gspec((None, 1, hg), lambda gi, c: (gi, 0, 0)),
        gspec((None, hg, 1), lambda gi, c: (gi, 0, 0)),
        gspec((None, 1, gw), lambda gi, c: (gi, 0, 0)),
        gspec((1, gw), lambda gi, c: (0, gi)),
    ]
    return pl.pallas_call(
        _ssd_kernel,
        out_shape=jax.ShapeDtypeStruct((s, d_inner), jnp.bfloat16),
        grid=(g, s // L),
        in_specs=in_specs,
        out_specs=pl.BlockSpec((L, gw), lambda gi, c: (c, gi)),
        scratch_shapes=[pltpu.VMEM((CONV_PAD + L, gw), jnp.float32),
                        pltpu.VMEM((CONV_PAD + L, n), jnp.float32),
                        pltpu.VMEM((CONV_PAD + L, n), jnp.float32),
                        pltpu.VMEM((n, gw), jnp.float32)],
        compiler_params=_cparams(("parallel", "arbitrary")),
        name="ssd",
    )(proj, proj, proj, proj, cw, cw, cw, cbias, cbias, cbias, dtc, dtr,
      dt_bias.reshape(g, 1, hg), dt_bias.reshape(g, hg, 1),
      a_log.reshape(g, 1, hg), a_log.reshape(g, hg, 1), dsk, norm_g.reshape(1, d_inner))


def _mamba_layer(x, g_mix, w_in_all, conv_w, conv_b, dt_bias, a_log, d_skip, norm_g, w_out_all, layer):
    d_inner = w_out_all.shape[1]
    n_main = 2 * d_inner + 2 * SSM_GROUPS * SSM_STATE
    n_dt = w_in_all.shape[2] - n_main
    h = _prenorm(x, g_mix)
    proj = _mm(h, w_in_all, layer=layer, n=n_main, out_dtype=jnp.bfloat16, tm=1024, tn=512,
               name="ssm_in_proj")
    dt_raw = _mm(h, w_in_all, layer=layer, col0=n_main, n=n_dt, out_dtype=jnp.float32, tm=1024, tn=n_dt,
                 name="ssm_dt_proj")
    y = _ssd(proj, dt_raw, conv_w, conv_b, dt_bias, a_log, d_skip, norm_g)
    return _mm(y, w_out_all, layer=layer, out_dtype=jnp.float32, tm=1024, tn=1024, tk=2048,
               mode="residual", extra=x, name="ssm_out_proj")


def _router_kernel(x_ref, g_ref, whi_ref, wlo_ref, oi_ref, of_ref, cnt_ref, hp_ref, carry):
    i = pl.program_id(0)
    tm = x_ref.shape[0]

    @pl.when(i == 0)
    def _():
        carry[...] = jnp.zeros_like(carry)

    x = x_ref[...]
    h = x * lax.rsqrt(jnp.mean(x * x, axis=-1, keepdims=True) + EPS) * g_ref[...]
    hp_ref[...] = _pack_bf16_pairs(h)
    hhi = _bf16(h)
    hlo = _bf16(h - hhi.astype(jnp.float32))
    whi = whi_ref[...]
    logits = _dot(hhi, whi) + (_dot(hlo, whi) + _dot(hhi, wlo_ref[...]))

    lane = lax.broadcasted_iota(jnp.int32, (tm, LANES), 1)
    ninf = -jnp.inf
    gmask = lane < N_GROUPS
    gl = jnp.where(gmask, logits, ninf)
    gmax = jnp.max(gl, axis=-1, keepdims=True)
    gsel = jnp.min(jnp.where(gl == gmax, lane, LANES), axis=-1, keepdims=True)
    psel = 1.0 / jnp.sum(jnp.where(gmask, jnp.exp(logits - gmax), 0.0), axis=-1, keepdims=True)
    lo = N_GROUPS + EXPERTS_PER_GROUP * gsel
    emask = (lane >= lo) & (lane < lo + EXPERTS_PER_GROUP)
    el = jnp.where(emask, logits, ninf)
    v1 = jnp.max(el, axis=-1, keepdims=True)
    i1 = jnp.min(jnp.where(el == v1, lane, LANES), axis=-1, keepdims=True)
    el2 = jnp.where(lane == i1, ninf, el)
    v2 = jnp.max(el2, axis=-1, keepdims=True)
    i2 = jnp.min(jnp.where(el2 == v2, lane, LANES), axis=-1, keepdims=True)
    e21 = jnp.exp(v2 - v1)
    w1 = psel / (1.0 + e21)
    w2 = psel * e21 / (1.0 + e21)

    hit1 = lane == i1
    hit2 = lane == i2
    onehot = jnp.where(hit1 | hit2, 1.0, 0.0)
    rid = lax.broadcasted_iota(jnp.int32, (tm, tm), 0)
    cid = lax.broadcasted_iota(jnp.int32, (tm, tm), 1)
    strict = _bf16(jnp.where(rid > cid, 1.0, 0.0))
    before = _dot(strict, _bf16(onehot)) + carry[...]
    r1 = jnp.sum(jnp.where(hit1, before, 0.0), axis=-1, keepdims=True)
    r2 = jnp.sum(jnp.where(hit2, before, 0.0), axis=-1, keepdims=True)
    carry[...] += jnp.sum(onehot, axis=0, keepdims=True)

    e1 = i1 - N_GROUPS
    e2 = i2 - N_GROUPS
    oi = jnp.where(lane == 0, e1, jnp.where(lane == 1, e2, jnp.where(
        lane == 2, r1.astype(jnp.int32), jnp.where(lane == 3, r2.astype(jnp.int32), 0))))
    oi_ref[...] = oi
    of_ref[...] = jnp.where(lane == 0, w1, jnp.where(lane == 1, w2, 0.0))
    cnt_ref[...] = carry[...]


def _router(x, g_ffn, w_group, w_expert, tm=256):
    s, d = x.shape
    wr = jnp.concatenate([w_group, w_expert], axis=1)
    wr = jnp.pad(wr, ((0, 0), (0, LANES - wr.shape[1])))
    whi = _bf16(wr)
    wlo = _bf16(wr - whi.astype(jnp.float32))
    return pl.pallas_call(
        _router_kernel,
        out_shape=(jax.ShapeDtypeStruct((s, LANES), jnp.int32),
                   jax.ShapeDtypeStruct((s, LANES), jnp.float32),
                   jax.ShapeDtypeStruct((1, LANES), jnp.float32),
                   jax.ShapeDtypeStruct((s, d // 2), jnp.uint32)),
        grid=(s // tm,),
        in_specs=[pl.BlockSpec((tm, d), lambda i: (i, 0)),
                  pl.BlockSpec((1, d), lambda i: (0, 0)),
                  pl.BlockSpec((d, LANES), lambda i: (0, 0)),
                  pl.BlockSpec((d, LANES), lambda i: (0, 0))],
        out_specs=(pl.BlockSpec((tm, LANES), lambda i: (i, 0)),
                   pl.BlockSpec((tm, LANES), lambda i: (i, 0)),
                   pl.BlockSpec((1, LANES), lambda i: (0, 0)),
                   pl.BlockSpec((tm, d // 2), lambda i: (i, 0))),
        scratch_shapes=[pltpu.VMEM((1, LANES), jnp.float32)],
        compiler_params=_cparams(("arbitrary",)),
        name="moe_router",
    )(x, g_ffn.reshape(1, d), whi, wlo)


def _invert_kernel(pos_ref, src_ref):
    def clear(r, carry):
        src_ref[r] = 0
        return carry

    def place(a, carry):
        src_ref[pos_ref[a]] = lax.shift_right_logical(a, 1)
        return carry

    lax.fori_loop(0, src_ref.shape[0], clear, 0, unroll=8)
    lax.fori_loop(0, pos_ref.shape[0], place, 0, unroll=8)


def _invert(pos_flat, n_rows):
    assert TOP_K == 2
    return pl.pallas_call(
        _invert_kernel,
        out_shape=jax.ShapeDtypeStruct((n_rows,), jnp.int32),
        in_specs=[pl.BlockSpec(memory_space=pltpu.SMEM)],
        out_specs=pl.BlockSpec(memory_space=pltpu.SMEM),
        name="moe_invert",
    )(pos_flat)


def _expert_kernel(te_ref, tv_ref, src_ref, x_hbm, wg_ref, wu_ref, wd_ref, ys_ref,
                   xbuf, wg16, wu16, wd16, sem):
    i = pl.program_id(0)
    nt = pl.num_programs(0)
    tm = xbuf.shape[1]

    def gather(tile, slot):
        def issue(r, carry):
            tok = src_ref[tile * tm + r]
            pltpu.make_async_copy(x_hbm.at[pl.ds(tok, 1)], xbuf.at[slot, pl.ds(r, 1)],
                                  sem.at[slot]).start()
            return carry
        lax.fori_loop(0, tm, issue, 0, unroll=8)

    @pl.when(i == 0)
    def _():
        gather(0, 0)

    nxt = jnp.minimum(i + 1, nt - 1)

    @pl.when((i + 1 < nt) & (tv_ref[nxt] > 0))
    def _():
        gather(i + 1, (i + 1) % 2)

    @pl.when((i == 0) | (te_ref[i] != te_ref[jnp.maximum(i - 1, 0)]))
    def _():
        wg16[...] = _bf16(wg_ref[...])
        wu16[...] = _bf16(wu_ref[...])
        wd16[...] = _bf16(wd_ref[...])

    @pl.when(tv_ref[i] > 0)
    def _():
        slot = i % 2
        pltpu.make_async_copy(x_hbm.at[pl.ds(0, tm)], xbuf.at[slot], sem.at[slot]).wait()
        half = xbuf.shape[2]
        h_hi, h_lo = (_bf16(t) for t in _unpack_bf16_pairs(xbuf[slot]))
        gate = _dot(h_hi, wg16[0:half, :]) + _dot(h_lo, wg16[half:, :])
        up = _dot(h_hi, wu16[0:half, :]) + _dot(h_lo, wu16[half:, :])
        hid = _bf16(_silu(gate) * up)
        ys_ref[...] = _pack_bf16_pairs(_dot(hid, wd16[...]))

    @pl.when(tv_ref[i] == 0)
    def _():
        ys_ref[...] = jnp.zeros_like(ys_ref)


def _experts(hp, src, w_gate, w_up, w_down, layer, tile_e, tile_v):
    dh = hp.shape[1]
    d = 2 * dh
    ff = w_gate.shape[-1]
    n_rows = src.shape[0]
    nt = n_rows // MOE_TM
    return pl.pallas_call(
        _expert_kernel,
        out_shape=jax.ShapeDtypeStruct((n_rows, dh), jnp.uint32),
        grid_spec=pltpu.PrefetchScalarGridSpec(
            num_scalar_prefetch=3,
            grid=(nt,),
            in_specs=[pl.BlockSpec(memory_space=pl.ANY),
                      pl.BlockSpec((None, None, d, ff), lambda i, te, tv, src: (layer, te[i], 0, 0)),
                      pl.BlockSpec((None, None, d, ff), lambda i, te, tv, src: (layer, te[i], 0, 0)),
                      pl.BlockSpec((None, None, ff, d), lambda i, te, tv, src: (layer, te[i], 0, 0))],
            out_specs=pl.BlockSpec((MOE_TM, dh), lambda i, te, tv, src: (i, 0)),
            scratch_shapes=[pltpu.VMEM((2, MOE_TM, dh), jnp.uint32),
                            pltpu.VMEM((d, ff), jnp.bfloat16),
                            pltpu.VMEM((d, ff), jnp.bfloat16),
                            pltpu.VMEM((ff, d), jnp.bfloat16),
                            pltpu.SemaphoreType.DMA((2,))]),
        compiler_params=_cparams(("arbitrary",)),
        name="moe_experts",
    )(tile_e, tile_v, src, hp, w_gate, w_up, w_down)


def _combine_kernel(pos_ref, x_ref, wt_ref, ys_hbm, o_ref, buf, sem, *, tc):
    i = pl.program_id(0)
    n = pl.num_programs(0)

    def gather(step, slot):
        def issue(t, carry):
            for k in range(TOP_K):
                p = pos_ref[TOP_K * (step * tc + t) + k]
                pltpu.make_async_copy(ys_hbm.at[pl.ds(p, 1)], buf.at[slot, k, pl.ds(t, 1)],
                                      sem.at[slot]).start()
            return carry
        lax.fori_loop(0, tc, issue, 0, unroll=8)

    @pl.when(i == 0)
    def _():
        gather(0, 0)

    @pl.when(i + 1 < n)
    def _():
        gather(i + 1, (i + 1) % 2)

    slot = i % 2
    for k in range(TOP_K):
        pltpu.make_async_copy(ys_hbm.at[pl.ds(0, tc)], buf.at[slot, k], sem.at[slot]).wait()
    half = buf.shape[3]
    rows_per_chunk = 32

    def chunk(c, carry):
        rows = pl.ds(pl.multiple_of(c * rows_per_chunk, rows_per_chunk), rows_per_chunk)
        wt = wt_ref[rows, :]
        y0_hi, y0_lo = _unpack_bf16_pairs(buf[slot, 0, rows, :])
        y1_hi, y1_lo = _unpack_bf16_pairs(buf[slot, 1, rows, :])
        o_ref[rows, 0:half] = x_ref[rows, 0:half] + wt[:, 0:1] * y0_hi + wt[:, 1:2] * y1_hi
        o_ref[rows, half:] = x_ref[rows, half:] + wt[:, 0:1] * y0_lo + wt[:, 1:2] * y1_lo
        return carry

    lax.fori_loop(0, tc // rows_per_chunk, chunk, 0)


def _combine(x, wts, ys, pos_flat, tc=256):
    s, d = x.shape
    return pl.pallas_call(
        functools.partial(_combine_kernel, tc=tc),
        out_shape=jax.ShapeDtypeStruct((s, d), jnp.float32),
        grid_spec=pltpu.PrefetchScalarGridSpec(
            num_scalar_prefetch=1,
            grid=(s // tc,),
            in_specs=[pl.BlockSpec((tc, d), lambda i, pos: (i, 0)),
                      pl.BlockSpec((tc, LANES), lambda i, pos: (i, 0)),
                      pl.BlockSpec(memory_space=pl.ANY)],
            out_specs=pl.BlockSpec((tc, d), lambda i, pos: (i, 0)),
            scratch_shapes=[pltpu.VMEM((2, TOP_K, tc, d // 2), jnp.uint32),
                            pltpu.SemaphoreType.DMA((2,))]),
        compiler_params=_cparams(("arbitrary",)),
        name="moe_combine",
    )(pos_flat, x, wts, ys)


def _moe_layer(x, g_ffn, w_group, w_expert, w_gate, w_up, w_down, layer):
    s = x.shape[0]
    oi, wts, cnt, hp = _router(x, g_ffn, w_group, w_expert)
    e = oi[:, 0:TOP_K]
    rank = oi[:, TOP_K:2 * TOP_K]
    counts = cnt[0, N_GROUPS:N_GROUPS + N_EXPERTS].astype(jnp.int32)
    tiles = (counts + MOE_TM - 1) // MOE_TM
    tile_end = jnp.cumsum(tiles)
    row_off = (tile_end - tiles) * MOE_TM
    pos = (row_off[e] + rank).reshape(-1).astype(jnp.int32)
    nt = s * TOP_K // MOE_TM + N_EXPERTS
    tidx = jnp.minimum(jnp.arange(nt, dtype=jnp.int32), tile_end[-1] - 1)
    tile_e = jnp.minimum(jnp.searchsorted(tile_end, tidx, side="right"), N_EXPERTS - 1)
    tile_v = (jnp.arange(nt, dtype=jnp.int32) < tile_end[-1]).astype(jnp.int32)
    src = _invert(pos, nt * MOE_TM)
    ys = _experts(hp, src, w_gate, w_up, w_down, layer, tile_e.astype(jnp.int32), tile_v)
    return _combine(x, wts, ys, pos)


def _rope_table_kernel(pos_ref, inv_ref, cos_ref, sin_ref):
    ang = pos_ref[...].astype(jnp.float32) * inv_ref[...]
    cos_ref[...] = jnp.cos(ang)
    sin_ref[...] = jnp.sin(ang)


def _rope_tables(pos_col, tm=512):
    s = pos_col.shape[0]
    half = ROPE_DIM // 2
    inv = jnp.power(ROPE_THETA, -jnp.arange(0, ROPE_DIM, 2, dtype=jnp.float32) / ROPE_DIM)
    inv = jnp.concatenate([inv, inv, jnp.zeros((LANES - 2 * half,), jnp.float32)]).reshape(1, LANES)
    return pl.pallas_call(
        _rope_table_kernel,
        out_shape=(jax.ShapeDtypeStruct((s, LANES), jnp.float32),) * 2,
        grid=(s // tm,),
        in_specs=[pl.BlockSpec((tm, 1), lambda i: (i, 0)),
                  pl.BlockSpec((1, LANES), lambda i: (0, 0))],
        out_specs=(pl.BlockSpec((tm, LANES), lambda i: (i, 0)),) * 2,
        compiler_params=_cparams(("parallel",)),
        name="rope_tables",
    )(pos_col, inv)


def _rot_half_matrix():
    half = ROPE_DIM // 2
    r = jnp.arange(LANES)[:, None]
    c = jnp.arange(LANES)[None, :]
    plus = (c >= half) & (c < ROPE_DIM) & (r == c - half)
    minus = (c < half) & (r == c + half)
    return _bf16(plus.astype(jnp.float32) - minus.astype(jnp.float32))


def _head_norm_rope(o_ref, h, nope, rope128, gn, gr128, cos, sin, rot_m, out_scale):
    ones = jnp.ones((LANES, LANES), jnp.bfloat16)
    ss = _dot(_bf16(nope * nope), ones) + _dot(_bf16(rope128 * rope128), ones)
    scale = lax.rsqrt(ss * (1.0 / QK_DIM) + EPS) * out_scale
    rg = rope128 * gr128
    roped = rg * cos + _dot(_bf16(rg), rot_m) * sin
    o_ref[h, :, 0:NOPE_DIM] = _bf16(nope * gn * scale)
    o_ref[h, :, NOPE_DIM:QK_DIM] = _bf16((roped * scale)[:, :ROPE_DIM])


def _kbuild_kernel(ckv_ref, w_ref, kr_ref, gn_ref, gr_ref, cos_ref, sin_ref, rot_ref, o_ref):
    kns = [_dot(ckv_ref[...], w_ref[h]) for h in range(w_ref.shape[0])]
    for h, kn in enumerate(kns):
        _head_norm_rope(o_ref, h, kn, kr_ref[...], gn_ref[...], gr_ref[...], cos_ref[...], sin_ref[...],
                        rot_ref[...], 1.0)


def _qbuild_kernel(qa_ref, w_ref, gn_ref, gr_ref, cos_ref, sin_ref, rot_ref, o_ref):
    qs = [_dot(qa_ref[...], w_ref[h]) for h in range(w_ref.shape[0])]
    for h, q in enumerate(qs):
        _head_norm_rope(o_ref, h, q[:, :NOPE_DIM], q[:, NOPE_DIM:], gn_ref[...], gr_ref[...],
                        cos_ref[...], sin_ref[...], rot_ref[...], Q_SCALE)


def _norm_gain_split(g):
    gn = g[:NOPE_DIM].reshape(1, NOPE_DIM)
    gr = jnp.pad(g[NOPE_DIM:], (0, LANES - ROPE_DIM)).reshape(1, LANES)
    return gn, gr


def _kbuild(ckv, w_uk, kr_raw, k_norm_g, cos, sin, tm=512, hb=HEADS_PER_STEP):
    s, r = ckv.shape
    tm = min(tm, s)
    heads = w_uk.shape[1] // NOPE_DIM
    w = _bf16(w_uk).reshape(r, heads, NOPE_DIM).transpose(1, 0, 2)
    gn, gr = _norm_gain_split(k_norm_g)
    return pl.pallas_call(
        _kbuild_kernel,
        out_shape=jax.ShapeDtypeStruct((heads, s, QK_DIM), jnp.bfloat16),
        grid=(s // tm, heads // hb),
        in_specs=[pl.BlockSpec((tm, r), lambda i, h: (i, 0)),
                  pl.BlockSpec((hb, r, NOPE_DIM), lambda i, h: (h, 0, 0)),
                  pl.BlockSpec((tm, LANES), lambda i, h: (i, 0)),
                  pl.BlockSpec((1, NOPE_DIM), lambda i, h: (0, 0)),
                  pl.BlockSpec((1, LANES), lambda i, h: (0, 0)),
                  pl.BlockSpec((tm, LANES), lambda i, h: (i, 0)),
                  pl.BlockSpec((tm, LANES), lambda i, h: (i, 0)),
                  pl.BlockSpec((LANES, LANES), lambda i, h: (0, 0))],
        out_specs=pl.BlockSpec((hb, tm, QK_DIM), lambda i, h: (h, i, 0)),
        compiler_params=_cparams(("parallel", "arbitrary")),
        name="mla_k_build",
    )(ckv, w, kr_raw, gn, gr, cos, sin, _rot_half_matrix())


def _qbuild(qa, w_uq, q_norm_g, cos, sin, tm=512, hb=HEADS_PER_STEP):
    s, r = qa.shape
    tm = min(tm, s)
    heads = w_uq.shape[1] // QK_DIM
    w = _bf16(w_uq).reshape(r, heads, QK_DIM).transpose(1, 0, 2)
    w = jnp.pad(w, ((0, 0), (0, 0), (0, 2 * LANES - QK_DIM)))
    gn, gr = _norm_gain_split(q_norm_g)
    return pl.pallas_call(
        _qbuild_kernel,
        out_shape=jax.ShapeDtypeStruct((heads, s, QK_DIM), jnp.bfloat16),
        grid=(s // tm, heads // hb),
        in_specs=[pl.BlockSpec((tm, r), lambda i, h: (i, 0)),
                  pl.BlockSpec((hb, r, 2 * LANES), lambda i, h: (h, 0, 0)),
                  pl.BlockSpec((1, NOPE_DIM), lambda i, h: (0, 0)),
                  pl.BlockSpec((1, LANES), lambda i, h: (0, 0)),
                  pl.BlockSpec((tm, LANES), lambda i, h: (i, 0)),
                  pl.BlockSpec((tm, LANES), lambda i, h: (i, 0)),
                  pl.BlockSpec((LANES, LANES), lambda i, h: (0, 0))],
        out_specs=pl.BlockSpec((hb, tm, QK_DIM), lambda i, h: (h, i, 0)),
        compiler_params=_cparams(("parallel", "arbitrary")),
        name="mla_q_build",
    )(qa, w, gn, gr, cos, sin, _rot_half_matrix())


def _reduce_rows(x, pair_op, final_op):
    rows = x.shape[0]
    while rows > 8:
        rows //= 2
        x = pair_op(x[:rows], x[rows:])
    return final_op(x, axis=0, keepdims=True)


def _attn_kernel(nfull_ref, nkv_ref, q_ref, k_ref, vt_ref, kpos_ref, qpos_ref, o_ref, *, tk, n_sub,
                 group_shift):
    i = pl.program_id(1)
    group = 1 << group_shift
    sub = q_ref.shape[0] // n_sub
    qs = [q_ref[h * sub:(h + 1) * sub, :] for h in range(n_sub)]
    qcs = [qpos_ref[:, h * sub:(h + 1) * sub] >> CHUNK_SHIFT for h in range(n_sub)]

    def tile_stats(js, masked):
        offs = [pl.multiple_of(j * tk, tk) for j in js]
        chains = [(t, h) for t in range(len(js)) for h in range(n_sub)]
        scores = {}
        for t, h in chains:
            k = k_ref[pl.ds(offs[t], tk), :]
            s = lax.dot_general(k, qs[h], (((1,), (1,)), ((), ())),
                                preferred_element_type=jnp.float32)
            if masked:
                kc = kpos_ref[pl.ds(offs[t], tk), :] >> CHUNK_SHIFT
                s = jnp.where(kc <= qcs[h], s, NEG_BIG)
            scores[t, h] = s
        soft = {}
        for t, h in chains:
            m_t = _reduce_rows(scores[t, h], jnp.maximum, jnp.max)
            p = jnp.exp2(scores[t, h] - m_t)
            soft[t, h] = (m_t, _reduce_rows(p, jnp.add, jnp.sum), _bf16(p))
        out = [[None] * n_sub for _ in js]
        for t, h in chains:
            vt = vt_ref[:, pl.ds(offs[t], tk)]
            out[t][h] = (soft[t, h][0], soft[t, h][1], _dot(vt, soft[t, h][2]))
        return out

    def merge(carry, tiles):
        out = []
        for h in range(n_sub):
            m, l, acc = carry[h]
            m_new = m
            for t in tiles:
                m_new = jnp.maximum(m_new, t[h][0])
            a = jnp.exp2(m - m_new)
            l, acc = l * a, acc * a
            for t in tiles:
                b = jnp.exp2(t[h][0] - m_new)
                l, acc = l + t[h][1] * b, acc + t[h][2] * b
            out.append((m_new, l, acc))
        return tuple(out)

    def group_step(jj, carry):
        return merge(carry, tile_stats([group * jj + u for u in range(group)], False))

    def single_step(j, carry):
        return merge(carry, tile_stats([j], False))

    def masked_step(j, carry):
        return merge(carry, tile_stats([j], True))

    init = tuple((jnp.full((1, sub), NEG_BIG, jnp.float32), jnp.zeros((1, sub), jnp.float32),
                  jnp.zeros((V_DIM, sub), jnp.float32)) for _ in range(n_sub))
    n_groups = lax.shift_right_logical(nfull_ref[i], group_shift)
    carry = lax.fori_loop(0, n_groups, group_step, init)
    carry = lax.fori_loop(group * n_groups, nfull_ref[i], single_step, carry)
    carry = lax.fori_loop(nfull_ref[i], nkv_ref[i], masked_step, carry)
    for h in range(n_sub):
        _, l, acc = carry[h]
        o_ref[h * sub:(h + 1) * sub, :] = _bf16((acc / l).T)


def _attention(q, k, vt, pos_col, pos_row, tq=512, tk=512, n_sub=2, group_shift=ATTN_GROUP_SHIFT):
    heads, s, _ = q.shape
    tq = min(tq, s)
    tk = min(tk, s)
    nfull, nkv = _kv_tile_ranges(pos_row.reshape(-1), tq, tk)
    return pl.pallas_call(
        functools.partial(_attn_kernel, tk=tk, n_sub=n_sub, group_shift=group_shift),
        out_shape=jax.ShapeDtypeStruct((s, heads * V_DIM), jnp.bfloat16),
        grid_spec=pltpu.PrefetchScalarGridSpec(
            num_scalar_prefetch=2,
            grid=(heads, s // tq),
            in_specs=[pl.BlockSpec((None, tq, QK_DIM), lambda h, i, nf, nk: (h, i, 0)),
                      pl.BlockSpec((None, s, QK_DIM), lambda h, i, nf, nk: (h, 0, 0)),
                      pl.BlockSpec((V_DIM, s), lambda h, i, nf, nk: (h, 0)),
                      pl.BlockSpec((s, 1), lambda h, i, nf, nk: (0, 0)),
                      pl.BlockSpec((1, tq), lambda h, i, nf, nk: (0, i))],
            out_specs=pl.BlockSpec((tq, V_DIM), lambda h, i, nf, nk: (i, h))),
        compiler_params=_cparams(("parallel", "arbitrary")),
        name="mla_attention",
    )(nfull, nkv, q, k, vt, pos_col, pos_row)


def _kv_tile_ranges(positions, tq, tk):
    cid = positions >> CHUNK_SHIFT
    qc = cid.reshape(-1, tq)
    kc = cid.reshape(-1, tk)
    qmin, qmax = jnp.min(qc, axis=1), jnp.max(qc, axis=1)
    kmin, kmax = jnp.min(kc, axis=1), jnp.max(kc, axis=1)
    vis = kmin[None, :] <= qmax[:, None]
    last = jnp.max(jnp.where(vis, jnp.arange(kmin.shape[0], dtype=jnp.int32)[None, :], -1), axis=1)
    full = (kmax[None, :] <= qmin[:, None]).astype(jnp.int32)
    nfull = jnp.sum(jnp.cumprod(full, axis=1), axis=1)
    return nfull.astype(jnp.int32), (last + 1).astype(jnp.int32)


def _shared_kv(x, cos, sin, kv_norm_g, w_dkv, kv_a_norm_g, w_kr, w_uk, w_uv, k_norm_g):
    hk = _prenorm(x, kv_norm_g)
    ckv = _mm(hk, _bf16(w_dkv), out_dtype=jnp.bfloat16, tm=1024, tn=w_dkv.shape[1],
              mode="rmsnorm", extra=kv_a_norm_g, name="mla_ckv")
    w_kr_pad = jnp.pad(_bf16(w_kr), ((0, 0), (0, LANES - ROPE_DIM)))
    kr_raw = _mm(hk, w_kr_pad, out_dtype=jnp.float32, tm=1024, tn=LANES, name="mla_krope")
    k = _kbuild(ckv, w_uk, kr_raw, k_norm_g, cos, sin)
    vt = _mm(ckv, _bf16(w_uv), out_dtype=jnp.bfloat16, tm=1024, tn=1024, mode="transpose",
             name="mla_v")
    return k, vt


def _mla_layer(x, g_mix, k, vt, cos, sin, pos_col, pos_row, w_dq_all, q_a_norm_g, w_uq, q_norm_g, w_o_all,
               layer, attn_group_shift):
    h = _prenorm(x, g_mix)
    qa = _mm(h, w_dq_all, layer=layer, out_dtype=jnp.bfloat16, tm=1024, tn=w_dq_all.shape[2], tk=2048,
             mode="rmsnorm", extra=q_a_norm_g, name="mla_qa")
    q = _qbuild(qa, w_uq, q_norm_g, cos, sin)
    o = _attention(q, k, vt, pos_col, pos_row, group_shift=attn_group_shift)
    return _mm(o, w_o_all, layer=layer, out_dtype=jnp.float32, tm=1024, tn=512, mode="residual", extra=x,
               name="mla_out_proj")


def kernel(x, positions, g_mix, g_ffn, ssm_w_in, ssm_conv_w, ssm_conv_b, ssm_dt_bias, ssm_a_log, ssm_d, ssm_norm_g, ssm_w_out, kv_norm_g, w_dkv, kv_a_norm_g, w_kr, w_uk, w_uv, k_norm_g, q_w_dq, q_a_norm_g, q_w_uq, q_norm_g, attn_w_o, moe_w_group, moe_w_expert, moe_w_gate, moe_w_up, moe_w_down):
    bsz, seq, d = x.shape
    assert bsz == 1
    depth = g_mix.shape[0]
    n_a = ssm_w_in.shape[0]
    xs = x.reshape(seq, d)
    pos = positions.reshape(seq).astype(jnp.int32)
    pos_col = pos.reshape(seq, 1)
    pos_row = pos.reshape(1, seq)
    cos, sin = _rope_tables(pos_col)
    k_shared = v_shared = None
    for layer in range(depth):
        if layer < n_a:
            i = layer
            xs = _mamba_layer(xs, g_mix[layer], ssm_w_in, ssm_conv_w[i], ssm_conv_b[i],
                              ssm_dt_bias[i], ssm_a_log[i], ssm_d[i], ssm_norm_g[i], ssm_w_out, i)
        else:
            j = layer - n_a
            xs = _mla_layer(xs, g_mix[layer], k_shared, v_shared, cos, sin, pos_col, pos_row,
                            q_w_dq, q_a_norm_g[j], q_w_uq[j], q_norm_g[j], attn_w_o, j,
                            ATTN_GROUP_SHIFT if j % 2 == 0 else ATTN_GROUP_SHIFT - 1)
        xs = _moe_layer(xs, g_ffn[layer], moe_w_group[layer], moe_w_expert[layer],
                        moe_w_gate, moe_w_up, moe_w_down, layer)
        if layer == n_a - 1:
            k_shared, v_shared = _shared_kv(xs, cos, sin, kv_norm_g, w_dkv, kv_a_norm_g, w_kr,
                                            w_uk, w_uv, k_norm_g)
    return xs.reshape(bsz, seq, d)
```

```python
import functools

import jax
import jax.numpy as jnp
from jax import lax
from jax.experimental import pallas as pl
from jax.experimental.pallas import tpu as pltpu

EPS = 1e-6
CHUNK = 64
CHUNK_SHIFT = 6

SSM_HEAD_DIM = 64
SSM_GROUPS = 8
SSM_STATE = 128
CONV_WIDTH = 4
SSD_L = 128
CONV_PAD = 8

NOPE_DIM = 128
ROPE_DIM = 64
QK_DIM = NOPE_DIM + ROPE_DIM
V_DIM = 128
ROPE_THETA = 10000.0

N_GROUPS = 4
EXPERTS_PER_GROUP = 8
N_EXPERTS = N_GROUPS * EXPERTS_PER_GROUP
TOP_K = 2
MOE_TM = 256

LANES = 128
VMEM_LIMIT_BYTES = 56 * 1024 * 1024
NEG_BIG = -1e30
LOG2E = 1.4426950408889634
Q_SCALE = QK_DIM ** -0.5 * LOG2E
HEADS_PER_STEP = 4
ATTN_GROUP_SHIFT = 2


def _cparams(sem):
    return pltpu.CompilerParams(dimension_semantics=sem, vmem_limit_bytes=VMEM_LIMIT_BYTES)


def _silu(v):
    hv = 0.5 * v
    return hv + hv * jnp.tanh(hv)


def _pack_bf16_pairs(v):
    c = v.shape[1] // 2
    bits = lax.bitcast_convert_type(_bf16(v).astype(jnp.float32), jnp.uint32)
    return bits[:, :c] | (bits[:, c:] >> 16)


def _unpack_bf16_pairs(u):
    hi = lax.bitcast_convert_type(u & jnp.uint32(0xFFFF0000), jnp.float32)
    lo = lax.bitcast_convert_type(u << 16, jnp.float32)
    return hi, lo


def _softplus(v):
    return jnp.maximum(v, 0.0) + jnp.log1p(jnp.exp(-jnp.abs(v)))


def _bf16(v):
    return v.astype(jnp.bfloat16)


def _dot(a, b):
    return jnp.dot(a, b, preferred_element_type=jnp.float32)


def _split3(v):
    hi = _bf16(v)
    r1 = v - hi.astype(jnp.float32)
    mid = _bf16(r1)
    lo = _bf16(r1 - mid.astype(jnp.float32))
    return hi, mid, lo


def _prenorm_kernel(x_ref, g_ref, o_ref):
    x = x_ref[...]
    y = x * lax.rsqrt(jnp.mean(x * x, axis=-1, keepdims=True) + EPS)
    o_ref[...] = _bf16(y * g_ref[...])


def _prenorm(x, g, tm=256):
    s, d = x.shape
    return pl.pallas_call(
        _prenorm_kernel,
        out_shape=jax.ShapeDtypeStruct((s, d), jnp.bfloat16),
        grid=(s // tm,),
        in_specs=[pl.BlockSpec((tm, d), lambda i: (i, 0)),
                  pl.BlockSpec((1, d), lambda i: (0, 0))],
        out_specs=pl.BlockSpec((tm, d), lambda i: (i, 0)),
        compiler_params=_cparams(("parallel",)),
        name="prenorm",
    )(x, g.reshape(1, d))


def _mm_kernel(*refs, nk, mode):
    a_ref, w_ref = refs[0], refs[1]
    if mode in ("plain", "transpose"):
        extra, o_ref, rest = None, refs[2], refs[3:]
    else:
        extra, o_ref, rest = refs[2], refs[3], refs[4:]

    def finish(acc):
        if mode == "transpose":
            acc = acc.T
        elif mode == "residual":
            acc = acc + extra[...]
        elif mode == "rmsnorm":
            acc = acc * lax.rsqrt(jnp.mean(acc * acc, axis=-1, keepdims=True) + EPS) * extra[...]
        o_ref[...] = acc.astype(o_ref.dtype)

    def product():
        return _dot(a_ref[...], _bf16(w_ref[...]))

    if nk == 1 and mode == "transpose":
        rest[0][...] = product()
        finish(rest[0][...])
        return
    if nk == 1:
        finish(product())
        return

    acc_ref = rest[0]
    k = pl.program_id(2)

    @pl.when(k == 0)
    def _():
        acc_ref[...] = jnp.zeros_like(acc_ref)

    acc_ref[...] += product()

    @pl.when(k == nk - 1)
    def _():
        finish(acc_ref[...])


def _mm(a, w, *, out_dtype, tm, tn, tk=None, mode="plain", extra=None, name="mm", layer=None,
        col0=0, n=None):
    m, kdim = a.shape
    n = w.shape[-1] - col0 if n is None else n
    tm = min(tm, m)
    tk = kdim if tk is None else tk
    nk = kdim // tk
    cb0 = col0 // tn
    assert col0 % tn == 0 and n % tn == 0 and kdim % tk == 0
    if layer is None:
        w_spec = pl.BlockSpec((tk, tn), lambda i, j, k: (k, j + cb0))
    else:
        w_spec = pl.BlockSpec((None, tk, tn), lambda i, j, k: (layer, k, j + cb0))
    in_specs = [pl.BlockSpec((tm, tk), lambda i, j, k: (i, k)), w_spec]
    args = [a, w]
    if mode == "residual":
        in_specs.append(pl.BlockSpec((tm, tn), lambda i, j, k: (i, j)))
        args.append(extra)
    elif mode == "rmsnorm":
        assert tn == n
        in_specs.append(pl.BlockSpec((1, tn), lambda i, j, k: (0, j)))
        args.append(extra.reshape(1, n))
    scratch = [pltpu.VMEM((tm, tn), jnp.float32)] if (nk > 1 or mode == "transpose") else []
    if mode == "transpose":
        out_shape = jax.ShapeDtypeStruct((n, m), out_dtype)
        out_spec = pl.BlockSpec((tn, tm), lambda i, j, k: (j, i))
    else:
        out_shape = jax.ShapeDtypeStruct((m, n), out_dtype)
        out_spec = pl.BlockSpec((tm, tn), lambda i, j, k: (i, j))
    return pl.pallas_call(
        functools.partial(_mm_kernel, nk=nk, mode=mode),
        out_shape=out_shape,
        grid=(m // tm, n // tn, nk),
        in_specs=in_specs,
        out_specs=out_spec,
        scratch_shapes=scratch,
        compiler_params=_cparams(("parallel", "parallel", "arbitrary")),
        name=name,
    )(*args)


def _pair_expand(cols, j, lane_lo):
    return jnp.where(lane_lo, cols[:, 2 * j:2 * j + 1], cols[:, 2 * j + 1:2 * j + 2])


def _ssd_kernel(z_ref, x_ref, b_ref, c_ref, cwx_ref, cwb_ref, cwc_ref, cbx_ref, cbb_ref, cbc_ref,
                dtc_ref, dtr_ref, biasc_ref, biasr_ref, alogc_ref, alogr_ref, dskip_ref, ng_ref,
                o_ref, xbuf, bbuf, cbuf, state):
    L = SSD_L
    c = pl.program_id(1)

    @pl.when(c == 0)
    def _():
        xbuf[0:CONV_PAD, :] = jnp.zeros((CONV_PAD, xbuf.shape[1]), jnp.float32)
        bbuf[0:CONV_PAD, :] = jnp.zeros((CONV_PAD, bbuf.shape[1]), jnp.float32)
        cbuf[0:CONV_PAD, :] = jnp.zeros((CONV_PAD, cbuf.shape[1]), jnp.float32)
        state[...] = jnp.zeros_like(state)

    def conv_silu(src_ref, buf, w_ref, bias_ref):
        buf[CONV_PAD:CONV_PAD + L, :] = src_ref[...].astype(jnp.float32)
        acc = bias_ref[...] + w_ref[CONV_WIDTH - 1:CONV_WIDTH, :] * buf[CONV_PAD:CONV_PAD + L, :]
        for k in range(CONV_WIDTH - 1):
            off = CONV_PAD - (CONV_WIDTH - 1) + k
            acc = acc + w_ref[k:k + 1, :] * buf[off:off + L, :]
        buf[0:CONV_PAD, :] = buf[L:L + CONV_PAD, :]
        return _silu(acc)

    xs = conv_silu(x_ref, xbuf, cwx_ref, cbx_ref)
    bs = conv_silu(b_ref, bbuf, cwb_ref, cbb_ref)
    cs = conv_silu(c_ref, cbuf, cwc_ref, cbc_ref)
    bs16 = _bf16(bs)
    cs16 = _bf16(cs)

    dtc = _softplus(dtc_ref[...] + biasc_ref[...])
    dtr = _softplus(dtr_ref[...] + biasr_ref[...])
    a_c = dtc * (-jnp.exp(alogc_ref[...]))
    a_r = dtr * (-jnp.exp(alogr_ref[...]))

    rid = lax.broadcasted_iota(jnp.int32, (L, L), 0)
    cid = lax.broadcasted_iota(jnp.int32, (L, L), 1)
    causal = rid >= cid
    tril = _bf16(jnp.where(causal, 1.0, 0.0))
    triu = _bf16(jnp.where(rid <= cid, 1.0, 0.0))
    acum_c = sum(_dot(tril, t) for t in _split3(a_c))
    acum_r = sum(_dot(t, triu) for t in _split3(a_r))

    total_c = acum_c[L - 1:L, :]
    e_c = jnp.exp(acum_c)
    w_c = dtc * jnp.exp(total_c - acum_c)
    etot = jnp.exp(total_c)

    cb = lax.dot_general(cs16, bs16, (((1,), (1,)), ((), ())),
                         preferred_element_type=jnp.float32)
    st = state[...]
    yoff = _dot(cs16, _bf16(st))

    lane = lax.broadcasted_iota(jnp.int32, (L, LANES), 1)
    lane_lo = lane < SSM_HEAD_DIM
    lane1_lo = lax.broadcasted_iota(jnp.int32, (1, LANES), 1) < SSM_HEAD_DIM

    z = z_ref[...].astype(jnp.float32)
    n_pairs = xs.shape[1] // LANES
    ys = []
    xws = []
    for j in range(n_pairs):
        xp = xs[:, j * LANES:(j + 1) * LANES]
        ms = []
        for hh in (2 * j, 2 * j + 1):
            seg = acum_c[:, hh:hh + 1] - acum_r[hh:hh + 1, :]
            dec = jnp.where(causal, jnp.exp(seg), 0.0)
            ms.append(_bf16(cb * dec * dtr[hh:hh + 1, :]))
        mpair = jnp.concatenate(ms, axis=1)
        rhs = jnp.concatenate([jnp.where(lane_lo, xp, 0.0), jnp.where(lane_lo, 0.0, xp)], axis=0)
        ydiag = _dot(mpair, _bf16(rhs))
        y = ydiag + yoff[:, j * LANES:(j + 1) * LANES] * _pair_expand(e_c, j, lane_lo)
        y = y + xp * dskip_ref[:, j * LANES:(j + 1) * LANES]
        ys.append(y)
        xws.append(_bf16(xp * _pair_expand(w_c, j, lane_lo)))
    y = jnp.concatenate(ys, axis=1)
    xw = jnp.concatenate(xws, axis=1)

    snew = lax.dot_general(bs16, xw, (((0,), (0,)), ((), ())),
                           preferred_element_type=jnp.float32)
    dec_row = jnp.concatenate([_pair_expand(etot, j, lane1_lo) for j in range(n_pairs)], axis=1)
    state[...] = st * dec_row + snew

    y = y * _silu(z)
    y = y * lax.rsqrt(jnp.mean(y * y, axis=-1, keepdims=True) + EPS)
    o_ref[...] = _bf16(y * ng_ref[...])


def _ssd(proj, dt_raw, conv_w, conv_b, dt_bias, a_log, d_skip, norm_g):
    s = proj.shape[0]
    g = SSM_GROUPS
    heads = dt_raw.shape[1]
    hg = heads // g
    gw = hg * SSM_HEAD_DIM
    d_inner = g * gw
    n = SSM_STATE
    L = SSD_L
    xb = d_inner // gw
    bb = 2 * d_inner // n
    cb_ = bb + g
    cwb = d_inner // n
    cwc = cwb + g

    dtc = dt_raw.reshape(s, g, hg).transpose(1, 0, 2)
    dtr = dt_raw.reshape(s, g, hg).transpose(1, 2, 0)
    cw = conv_w.reshape(CONV_WIDTH, -1)
    cbias = conv_b.reshape(1, -1)
    dsk = jnp.repeat(d_skip.reshape(g, 1, hg), SSM_HEAD_DIM, axis=2)

    def gspec(shape, fn):
        return pl.BlockSpec(shape, fn)

    in_specs = [
        gspec((L, gw), lambda gi, c: (c, gi)),
        gspec((L, gw), lambda gi, c: (c, xb + gi)),
        gspec((L, n), lambda gi, c: (c, bb + gi)),
        gspec((L, n), lambda gi, c: (c, cb_ + gi)),
        gspec((CONV_WIDTH, gw), lambda gi, c: (0, gi)),
        gspec((CONV_WIDTH, n), lambda gi, c: (0, cwb + gi)),
        gspec((CONV_WIDTH, n), lambda gi, c: (0, cwc + gi)),
        gspec((1, gw), lambda gi, c: (0, gi)),
        gspec((1, n), lambda gi, c: (0, cwb + gi)),
        gspec((1, n), lambda gi, c: (0, cwc + gi)),
        gspec((None, L, hg), lambda gi, c: (gi, c, 0)),
        gspec((None, hg, L), lambda gi, c: (gi, 0, c)),
        gspec((None, 1, hg), lambda gi, c: (gi, 0, 0)),
        gspec((None, hg, 1), lambda gi, c: (gi, 0, 0)),
        gspec((None, 1, hg), lambda gi, c: (gi, 0, 0)),
        gspec((None, hg, 1), lambda gi, c: (gi, 0, 0)),
        gspec((None, 1, gw), lambda gi, c: (gi, 0, 0)),
        gspec((1, gw), lambda gi, c: (0, gi)),
    ]
    return pl.pallas_call(
        _ssd_kernel,
        out_shape=jax.ShapeDtypeStruct((s, d_inner), jnp.bfloat16),
        grid=(g, s // L),
        in_specs=in_specs,
        out_specs=pl.BlockSpec((L, gw), lambda gi, c: (c, gi)),
        scratch_shapes=[pltpu.VMEM((CONV_PAD + L, gw), jnp.float32),
                        pltpu.VMEM((CONV_PAD + L, n), jnp.float32),
                        pltpu.VMEM((CONV_PAD + L, n), jnp.float32),
                        pltpu.VMEM((n, gw), jnp.float32)],
        compiler_params=_cparams(("parallel", "arbitrary")),
        name="ssd",
    )(proj, proj, proj, proj, cw, cw, cw, cbias, cbias, cbias, dtc, dtr,
      dt_bias.reshape(g, 1, hg), dt_bias.reshape(g, hg, 1),
      a_log.reshape(g, 1, hg), a_log.reshape(g, hg, 1), dsk, norm_g.reshape(1, d_inner))


def _mamba_layer(x, h, w_in_all, conv_w, conv_b, dt_bias, a_log, d_skip, norm_g, w_out_all, layer):
    d_inner = w_out_all.shape[1]
    n_main = 2 * d_inner + 2 * SSM_GROUPS * SSM_STATE
    n_dt = w_in_all.shape[2] - n_main
    proj = _mm(h, w_in_all, layer=layer, n=n_main, out_dtype=jnp.bfloat16, tm=1024, tn=512,
               name="ssm_in_proj")
    dt_raw = _mm(h, w_in_all, layer=layer, col0=n_main, n=n_dt, out_dtype=jnp.float32, tm=1024, tn=n_dt,
                 name="ssm_dt_proj")
    y = _ssd(proj, dt_raw, conv_w, conv_b, dt_bias, a_log, d_skip, norm_g)
    return _mm(y, w_out_all, layer=layer, out_dtype=jnp.float32, tm=1024, tn=1024, tk=2048,
               mode="residual", extra=x, name="ssm_out_proj")


def _router_kernel(x_ref, g_ref, whi_ref, wlo_ref, oi_ref, of_ref, cnt_ref, hp_ref, carry):
    i = pl.program_id(0)
    tm = x_ref.shape[0]

    @pl.when(i == 0)
    def _():
        carry[...] = jnp.zeros_like(carry)

    x = x_ref[...]
    h = x * lax.rsqrt(jnp.mean(x * x, axis=-1, keepdims=True) + EPS) * g_ref[...]
    hp_ref[...] = _pack_bf16_pairs(h)
    hhi = _bf16(h)
    hlo = _bf16(h - hhi.astype(jnp.float32))
    whi = whi_ref[...]
    logits = _dot(hhi, whi) + (_dot(hlo, whi) + _dot(hhi, wlo_ref[...]))

    lane = lax.broadcasted_iota(jnp.int32, (tm, LANES), 1)
    ninf = -jnp.inf
    gmask = lane < N_GROUPS
    gl = jnp.where(gmask, logits, ninf)
    gmax = jnp.max(gl, axis=-1, keepdims=True)
    gsel = jnp.min(jnp.where(gl == gmax, lane, LANES), axis=-1, keepdims=True)
    psel = 1.0 / jnp.sum(jnp.where(gmask, jnp.exp(logits - gmax), 0.0), axis=-1, keepdims=True)
    lo = N_GROUPS + EXPERTS_PER_GROUP * gsel
    emask = (lane >= lo) & (lane < lo + EXPERTS_PER_GROUP)
    el = jnp.where(emask, logits, ninf)
    v1 = jnp.max(el, axis=-1, keepdims=True)
    i1 = jnp.min(jnp.where(el == v1, lane, LANES), axis=-1, keepdims=True)
    el2 = jnp.where(lane == i1, ninf, el)
    v2 = jnp.max(el2, axis=-1, keepdims=True)
    i2 = jnp.min(jnp.where(el2 == v2, lane, LANES), axis=-1, keepdims=True)
    e21 = jnp.exp(v2 - v1)
    w1 = psel / (1.0 + e21)
    w2 = psel * e21 / (1.0 + e21)

    hit1 = lane == i1
    hit2 = lane == i2
    onehot = jnp.where(hit1 | hit2, 1.0, 0.0)
    rid = lax.broadcasted_iota(jnp.int32, (tm, tm), 0)
    cid = lax.broadcasted_iota(jnp.int32, (tm, tm), 1)
    strict = _bf16(jnp.where(rid > cid, 1.0, 0.0))
    before = _dot(strict, _bf16(onehot)) + carry[...]
    r1 = jnp.sum(jnp.where(hit1, before, 0.0), axis=-1, keepdims=True)
    r2 = jnp.sum(jnp.where(hit2, before, 0.0), axis=-1, keepdims=True)
    carry[...] += jnp.sum(onehot, axis=0, keepdims=True)

    e1 = i1 - N_GROUPS
    e2 = i2 - N_GROUPS
    oi = jnp.where(lane == 0, e1, jnp.where(lane == 1, e2, jnp.where(
        lane == 2, r1.astype(jnp.int32), jnp.where(lane == 3, r2.astype(jnp.int32), 0))))
    oi_ref[...] = oi
    of_ref[...] = jnp.where(lane == 0, w1, jnp.where(lane == 1, w2, 0.0))
    cnt_ref[...] = carry[...]


def _router(x, g_ffn, w_group, w_expert, tm=256):
    s, d = x.shape
    wr = jnp.concatenate([w_group, w_expert], axis=1)
    wr = jnp.pad(wr, ((0, 0), (0, LANES - wr.shape[1])))
    whi = _bf16(wr)
    wlo = _bf16(wr - whi.astype(jnp.float32))
    return pl.pallas_call(
        _router_kernel,
        out_shape=(jax.ShapeDtypeStruct((s, LANES), jnp.int32),
                   jax.ShapeDtypeStruct((s, LANES), jnp.float32),
                   jax.ShapeDtypeStruct((1, LANES), jnp.float32),
                   jax.ShapeDtypeStruct((s, d // 2), jnp.uint32)),
        grid=(s // tm,),
        in_specs=[pl.BlockSpec((tm, d), lambda i: (i, 0)),
                  pl.BlockSpec((1, d), lambda i: (0, 0)),
                  pl.BlockSpec((d, LANES), lambda i: (0, 0)),
                  pl.BlockSpec((d, LANES), lambda i: (0, 0))],
        out_specs=(pl.BlockSpec((tm, LANES), lambda i: (i, 0)),
                   pl.BlockSpec((tm, LANES), lambda i: (i, 0)),
                   pl.BlockSpec((1, LANES), lambda i: (0, 0)),
                   pl.BlockSpec((tm, d // 2), lambda i: (i, 0))),
        scratch_shapes=[pltpu.VMEM((1, LANES), jnp.float32)],
        compiler_params=_cparams(("arbitrary",)),
        name="moe_router",
    )(x, g_ffn.reshape(1, d), whi, wlo)


def _invert_kernel(pos_ref, src_ref):
    def clear(r, carry):
        src_ref[r] = 0
        return carry

    def place(a, carry):
        src_ref[pos_ref[a]] = lax.shift_right_logical(a, 1)
        return carry

    lax.fori_loop(0, src_ref.shape[0], clear, 0, unroll=8)
    lax.fori_loop(0, pos_ref.shape[0], place, 0, unroll=8)


def _invert(pos_flat, n_rows):
    assert TOP_K == 2
    return pl.pallas_call(
        _invert_kernel,
        out_shape=jax.ShapeDtypeStruct((n_rows,), jnp.int32),
        in_specs=[pl.BlockSpec(memory_space=pltpu.SMEM)],
        out_specs=pl.BlockSpec(memory_space=pltpu.SMEM),
        name="moe_invert",
    )(pos_flat)


def _expert_kernel(te_ref, tv_ref, first_ref, nexte_ref, wslot_ref, src_ref, x_hbm, wg_hbm, wu_hbm, wd_hbm,
                   ys_ref, xbuf, wbuf_g, wbuf_u, wbuf_d, wg16, wu16, wd16, sem, wsem, *, layer):
    i = pl.program_id(0)
    nt = pl.num_programs(0)
    tm = xbuf.shape[1]

    def gather(tile, slot):
        def issue(r, carry):
            tok = src_ref[tile * tm + r]
            pltpu.make_async_copy(x_hbm.at[pl.ds(tok, 1)], xbuf.at[slot, pl.ds(r, 1)],
                                  sem.at[slot]).start()
            return carry
        lax.fori_loop(0, tm, issue, 0, unroll=8)

    def weight_copies(e, slot):
        return (pltpu.make_async_copy(wg_hbm.at[layer, e], wbuf_g.at[slot], wsem.at[slot, 0]),
                pltpu.make_async_copy(wu_hbm.at[layer, e], wbuf_u.at[slot], wsem.at[slot, 1]),
                pltpu.make_async_copy(wd_hbm.at[layer, e], wbuf_d.at[slot], wsem.at[slot, 2]))

    @pl.when(i == 0)
    def _():
        gather(0, 0)
        for c in weight_copies(te_ref[0], 0):
            c.start()

    nxt = jnp.minimum(i + 1, nt - 1)

    @pl.when((i + 1 < nt) & (tv_ref[nxt] > 0))
    def _():
        gather(i + 1, (i + 1) % 2)

    @pl.when(first_ref[i] > 0)
    def _():
        ws = wslot_ref[i]
        for c in weight_copies(te_ref[i], ws):
            c.wait()
        ne = nexte_ref[i]

        @pl.when(ne >= 0)
        def _():
            for c in weight_copies(ne, 1 - ws):
                c.start()

        wg16[...] = _bf16(wbuf_g[ws])
        wu16[...] = _bf16(wbuf_u[ws])
        wd16[...] = _bf16(wbuf_d[ws])

    @pl.when(tv_ref[i] > 0)
    def _():
        slot = i % 2
        pltpu.make_async_copy(x_hbm.at[pl.ds(0, tm)], xbuf.at[slot], sem.at[slot]).wait()
        half = xbuf.shape[2]
        h_hi, h_lo = (_bf16(t) for t in _unpack_bf16_pairs(xbuf[slot]))
        gate = _dot(h_hi, wg16[0:half, :]) + _dot(h_lo, wg16[half:, :])
        up = _dot(h_hi, wu16[0:half, :]) + _dot(h_lo, wu16[half:, :])
        hid = _bf16(_silu(gate) * up)
        ys_ref[...] = _pack_bf16_pairs(_dot(hid, wd16[...]))

    @pl.when(tv_ref[i] == 0)
    def _():
        ys_ref[...] = jnp.zeros_like(ys_ref)


def _experts(hp, src, w_gate, w_up, w_down, layer, tile_e, tile_v, counts):
    dh = hp.shape[1]
    d = 2 * dh
    ff = w_gate.shape[-1]
    n_rows = src.shape[0]
    nt = n_rows // MOE_TM
    idx = jnp.arange(nt, dtype=jnp.int32)
    prev_e = jnp.concatenate([jnp.full((1,), -1, jnp.int32), tile_e[:-1]])
    first = ((tile_v > 0) & ((idx == 0) | (tile_e != prev_e))).astype(jnp.int32)
    wslot = (jnp.cumsum(first) - 1) & 1
    eids = jnp.arange(N_EXPERTS, dtype=jnp.int32)
    present = jnp.where(counts > 0, eids, N_EXPERTS)
    at_or_after = lax.cummin(present, reverse=True)
    after = jnp.concatenate([at_or_after[1:], jnp.full((1,), N_EXPERTS, jnp.int32)])
    after = jnp.where(after >= N_EXPERTS, -1, after)
    nexte = jnp.where(first > 0, after[tile_e], -1).astype(jnp.int32)
    any_spec = pl.BlockSpec(memory_space=pl.ANY)
    return pl.pallas_call(
        functools.partial(_expert_kernel, layer=layer),
        out_shape=jax.ShapeDtypeStruct((n_rows, dh), jnp.uint32),
        grid_spec=pltpu.PrefetchScalarGridSpec(
            num_scalar_prefetch=6,
            grid=(nt,),
            in_specs=[any_spec, any_spec, any_spec, any_spec],
            out_specs=pl.BlockSpec((MOE_TM, dh), lambda i, *_: (i, 0)),
            scratch_shapes=[pltpu.VMEM((2, MOE_TM, dh), jnp.uint32),
                            pltpu.VMEM((2, d, ff), jnp.float32),
                            pltpu.VMEM((2, d, ff), jnp.float32),
                            pltpu.VMEM((2, ff, d), jnp.float32),
                            pltpu.VMEM((d, ff), jnp.bfloat16),
                            pltpu.VMEM((d, ff), jnp.bfloat16),
                            pltpu.VMEM((ff, d), jnp.bfloat16),
                            pltpu.SemaphoreType.DMA((2,)),
                            pltpu.SemaphoreType.DMA((2, 3))]),
        compiler_params=_cparams(("arbitrary",)),
        name="moe_experts",
    )(tile_e, tile_v, first, nexte, wslot.astype(jnp.int32), src, hp, w_gate, w_up, w_down)


def _combine_kernel(pos_ref, x_ref, wt_ref, ys_hbm, *rest, tc, n_norm):
    gain_refs = rest[:n_norm]
    o_ref = rest[n_norm]
    h_refs = rest[n_norm + 1:2 * n_norm + 1]
    buf, sem = rest[2 * n_norm + 1:]
    i = pl.program_id(0)
    n = pl.num_programs(0)

    def gather(step, slot):
        def issue(t, carry):
            for k in range(TOP_K):
                p = pos_ref[TOP_K * (step * tc + t) + k]
                pltpu.make_async_copy(ys_hbm.at[pl.ds(p, 1)], buf.at[slot, k, pl.ds(t, 1)],
                                      sem.at[slot]).start()
            return carry
        lax.fori_loop(0, tc, issue, 0, unroll=8)

    @pl.when(i == 0)
    def _():
        gather(0, 0)

    @pl.when(i + 1 < n)
    def _():
        gather(i + 1, (i + 1) % 2)

    slot = i % 2
    for k in range(TOP_K):
        pltpu.make_async_copy(ys_hbm.at[pl.ds(0, tc)], buf.at[slot, k], sem.at[slot]).wait()
    half = buf.shape[3]
    rows_per_chunk = 32

    def chunk(c, carry):
        rows = pl.ds(pl.multiple_of(c * rows_per_chunk, rows_per_chunk), rows_per_chunk)
        wt = wt_ref[rows, :]
        y0_hi, y0_lo = _unpack_bf16_pairs(buf[slot, 0, rows, :])
        y1_hi, y1_lo = _unpack_bf16_pairs(buf[slot, 1, rows, :])
        out_hi = x_ref[rows, 0:half] + wt[:, 0:1] * y0_hi + wt[:, 1:2] * y1_hi
        out_lo = x_ref[rows, half:] + wt[:, 0:1] * y0_lo + wt[:, 1:2] * y1_lo
        o_ref[rows, 0:half] = out_hi
        o_ref[rows, half:] = out_lo
        if n_norm:
            ss = (jnp.sum(out_hi * out_hi, axis=-1, keepdims=True)
                  + jnp.sum(out_lo * out_lo, axis=-1, keepdims=True))
            scale = lax.rsqrt(ss * (1.0 / (2 * half)) + EPS)
            for g_ref, h_ref in zip(gain_refs, h_refs):
                h_ref[rows, 0:half] = _bf16(out_hi * scale * g_ref[:, 0:half])
                h_ref[rows, half:] = _bf16(out_lo * scale * g_ref[:, half:])
        return carry

    lax.fori_loop(0, tc // rows_per_chunk, chunk, 0)


def _combine(x, wts, ys, pos_flat, gains, tc=256):
    s, d = x.shape
    n_norm = len(gains)
    row_spec = pl.BlockSpec((tc, d), lambda i, pos: (i, 0))
    gain_spec = pl.BlockSpec((1, d), lambda i, pos: (0, 0))
    outs = pl.pallas_call(
        functools.partial(_combine_kernel, tc=tc, n_norm=n_norm),
        out_shape=(jax.ShapeDtypeStruct((s, d), jnp.float32),)
        + (jax.ShapeDtypeStruct((s, d), jnp.bfloat16),) * n_norm,
        grid_spec=pltpu.PrefetchScalarGridSpec(
            num_scalar_prefetch=1,
            grid=(s // tc,),
            in_specs=[row_spec, pl.BlockSpec((tc, LANES), lambda i, pos: (i, 0)),
                      pl.BlockSpec(memory_space=pl.ANY)] + [gain_spec] * n_norm,
            out_specs=(row_spec,) * (1 + n_norm),
            scratch_shapes=[pltpu.VMEM((2, TOP_K, tc, d // 2), jnp.uint32),
                            pltpu.SemaphoreType.DMA((2,))]),
        compiler_params=_cparams(("arbitrary",)),
        name="moe_combine",
    )(pos_flat, x, wts, ys, *[g.reshape(1, d) for g in gains])
    return outs[0], list(outs[1:])


def _moe_layer(x, g_ffn, w_group, w_expert, w_gate, w_up, w_down, layer, next_gains):
    s = x.shape[0]
    oi, wts, cnt, hp = _router(x, g_ffn, w_group, w_expert)
    e = oi[:, 0:TOP_K]
    rank = oi[:, TOP_K:2 * TOP_K]
    counts = cnt[0, N_GROUPS:N_GROUPS + N_EXPERTS].astype(jnp.int32)
    tiles = (counts + MOE_TM - 1) // MOE_TM
    tile_end = jnp.cumsum(tiles)
    row_off = (tile_end - tiles) * MOE_TM
    pos = (row_off[e] + rank).reshape(-1).astype(jnp.int32)
    nt = s * TOP_K // MOE_TM + N_EXPERTS
    tidx = jnp.minimum(jnp.arange(nt, dtype=jnp.int32), tile_end[-1] - 1)
    tile_e = jnp.minimum(jnp.searchsorted(tile_end, tidx, side="right"), N_EXPERTS - 1)
    tile_v = (jnp.arange(nt, dtype=jnp.int32) < tile_end[-1]).astype(jnp.int32)
    src = _invert(pos, nt * MOE_TM)
    ys = _experts(hp, src, w_gate, w_up, w_down, layer, tile_e.astype(jnp.int32), tile_v, counts)
    return _combine(x, wts, ys, pos, next_gains)


def _rope_table_kernel(pos_ref, inv_ref, cos_ref, sin_ref):
    ang = pos_ref[...].astype(jnp.float32) * inv_ref[...]
    cos_ref[...] = jnp.cos(ang)
    sin_ref[...] = jnp.sin(ang)


def _rope_tables(pos_col, tm=512):
    s = pos_col.shape[0]
    half = ROPE_DIM // 2
    inv = jnp.power(ROPE_THETA, -jnp.arange(0, ROPE_DIM, 2, dtype=jnp.float32) / ROPE_DIM)
    inv = jnp.concatenate([inv, inv, jnp.zeros((LANES - 2 * half,), jnp.float32)]).reshape(1, LANES)
    return pl.pallas_call(
        _rope_table_kernel,
        out_shape=(jax.ShapeDtypeStruct((s, LANES), jnp.float32),) * 2,
        grid=(s // tm,),
        in_specs=[pl.BlockSpec((tm, 1), lambda i: (i, 0)),
                  pl.BlockSpec((1, LANES), lambda i: (0, 0))],
        out_specs=(pl.BlockSpec((tm, LANES), lambda i: (i, 0)),) * 2,
        compiler_params=_cparams(("parallel",)),
        name="rope_tables",
    )(pos_col, inv)


def _rot_half_matrix():
    half = ROPE_DIM // 2
    r = jnp.arange(LANES)[:, None]
    c = jnp.arange(LANES)[None, :]
    plus = (c >= half) & (c < ROPE_DIM) & (r == c - half)
    minus = (c < half) & (r == c + half)
    return _bf16(plus.astype(jnp.float32) - minus.astype(jnp.float32))


def _head_norm_rope(o_ref, h, nope, rope128, gn, gr128, cos, sin, rot_m, out_scale):
    ones = jnp.ones((LANES, LANES), jnp.bfloat16)
    ss = _dot(_bf16(nope * nope), ones) + _dot(_bf16(rope128 * rope128), ones)
    scale = lax.rsqrt(ss * (1.0 / QK_DIM) + EPS) * out_scale
    rg = rope128 * gr128
    roped = rg * cos + _dot(_bf16(rg), rot_m) * sin
    o_ref[h, :, 0:NOPE_DIM] = _bf16(nope * gn * scale)
    o_ref[h, :, NOPE_DIM:QK_DIM] = _bf16((roped * scale)[:, :ROPE_DIM])


def _kbuild_kernel(ckv_ref, w_ref, kr_ref, gn_ref, gr_ref, cos_ref, sin_ref, rot_ref, o_ref):
    kns = [_dot(ckv_ref[...], w_ref[h]) for h in range(w_ref.shape[0])]
    for h, kn in enumerate(kns):
        _head_norm_rope(o_ref, h, kn, kr_ref[...], gn_ref[...], gr_ref[...], cos_ref[...], sin_ref[...],
                        rot_ref[...], 1.0)


def _qbuild_kernel(qa_ref, w_ref, gn_ref, gr_ref, cos_ref, sin_ref, rot_ref, o_ref):
    qs = [_dot(qa_ref[...], w_ref[h]) for h in range(w_ref.shape[0])]
    for h, q in enumerate(qs):
        _head_norm_rope(o_ref, h, q[:, :NOPE_DIM], q[:, NOPE_DIM:], gn_ref[...], gr_ref[...],
                        cos_ref[...], sin_ref[...], rot_ref[...], Q_SCALE)


def _norm_gain_split(g):
    gn = g[:NOPE_DIM].reshape(1, NOPE_DIM)
    gr = jnp.pad(g[NOPE_DIM:], (0, LANES - ROPE_DIM)).reshape(1, LANES)
    return gn, gr


def _kbuild(ckv, w_uk, kr_raw, k_norm_g, cos, sin, tm=512, hb=HEADS_PER_STEP):
    s, r = ckv.shape
    tm = min(tm, s)
    heads = w_uk.shape[1] // NOPE_DIM
    w = _bf16(w_uk).reshape(r, heads, NOPE_DIM).transpose(1, 0, 2)
    gn, gr = _norm_gain_split(k_norm_g)
    return pl.pallas_call(
        _kbuild_kernel,
        out_shape=jax.ShapeDtypeStruct((heads, s, QK_DIM), jnp.bfloat16),
        grid=(s // tm, heads // hb),
        in_specs=[pl.BlockSpec((tm, r), lambda i, h: (i, 0)),
                  pl.BlockSpec((hb, r, NOPE_DIM), lambda i, h: (h, 0, 0)),
                  pl.BlockSpec((tm, LANES), lambda i, h: (i, 0)),
                  pl.BlockSpec((1, NOPE_DIM), lambda i, h: (0, 0)),
                  pl.BlockSpec((1, LANES), lambda i, h: (0, 0)),
                  pl.BlockSpec((tm, LANES), lambda i, h: (i, 0)),
                  pl.BlockSpec((tm, LANES), lambda i, h: (i, 0)),
                  pl.BlockSpec((LANES, LANES), lambda i, h: (0, 0))],
        out_specs=pl.BlockSpec((hb, tm, QK_DIM), lambda i, h: (h, i, 0)),
        compiler_params=_cparams(("parallel", "arbitrary")),
        name="mla_k_build",
    )(ckv, w, kr_raw, gn, gr, cos, sin, _rot_half_matrix())


def _qbuild(qa, w_uq, q_norm_g, cos, sin, tm=512, hb=HEADS_PER_STEP):
    s, r = qa.shape
    tm = min(tm, s)
    heads = w_uq.shape[1] // QK_DIM
    w = _bf16(w_uq).reshape(r, heads, QK_DIM).transpose(1, 0, 2)
    w = jnp.pad(w, ((0, 0), (0, 0), (0, 2 * LANES - QK_DIM)))
    gn, gr = _norm_gain_split(q_norm_g)
    return pl.pallas_call(
        _qbuild_kernel,
        out_shape=jax.ShapeDtypeStruct((heads, s, QK_DIM), jnp.bfloat16),
        grid=(s // tm, heads // hb),
        in_specs=[pl.BlockSpec((tm, r), lambda i, h: (i, 0)),
                  pl.BlockSpec((hb, r, 2 * LANES), lambda i, h: (h, 0, 0)),
                  pl.BlockSpec((1, NOPE_DIM), lambda i, h: (0, 0)),
                  pl.BlockSpec((1, LANES), lambda i, h: (0, 0)),
                  pl.BlockSpec((tm, LANES), lambda i, h: (i, 0)),
                  pl.BlockSpec((tm, LANES), lambda i, h: (i, 0)),
                  pl.BlockSpec((LANES, LANES), lambda i, h: (0, 0))],
        out_specs=pl.BlockSpec((hb, tm, QK_DIM), lambda i, h: (h, i, 0)),
        compiler_params=_cparams(("parallel", "arbitrary")),
        name="mla_q_build",
    )(qa, w, gn, gr, cos, sin, _rot_half_matrix())


def _reduce_rows(x, pair_op, final_op):
    rows = x.shape[0]
    while rows > 8:
        rows //= 2
        x = pair_op(x[:rows], x[rows:])
    return final_op(x, axis=0, keepdims=True)


def _attn_kernel(nfull_ref, nkv_ref, q_ref, k_ref, vt_ref, kpos_ref, qpos_ref, o_ref, *, tk, n_sub,
                 group_shift):
    i = pl.program_id(1)
    group = 1 << group_shift
    sub = q_ref.shape[0] // n_sub
    qs = [q_ref[h * sub:(h + 1) * sub, :] for h in range(n_sub)]
    qcs = [qpos_ref[:, h * sub:(h + 1) * sub] >> CHUNK_SHIFT for h in range(n_sub)]

    def tile_stats(pieces, masked):
        chains = [(t, h) for t in range(len(pieces)) for h in range(n_sub)]
        scores = {}
        for t, h in chains:
            off, size = pieces[t]
            k = k_ref[pl.ds(off, size), :]
            s = lax.dot_general(k, qs[h], (((1,), (1,)), ((), ())),
                                preferred_element_type=jnp.float32)
            if masked:
                kc = kpos_ref[pl.ds(off, size), :] >> CHUNK_SHIFT
                s = jnp.where(kc <= qcs[h], s, NEG_BIG)
            scores[t, h] = s
        soft = {}
        for t, h in chains:
            m_t = _reduce_rows(scores[t, h], jnp.maximum, jnp.max)
            p = jnp.exp2(scores[t, h] - m_t)
            soft[t, h] = (m_t, _reduce_rows(p, jnp.add, jnp.sum), _bf16(p))
        out = [[None] * n_sub for _ in pieces]
        for t, h in chains:
            off, size = pieces[t]
            vt = vt_ref[:, pl.ds(off, size)]
            out[t][h] = (soft[t, h][0], soft[t, h][1], _dot(vt, soft[t, h][2]))
        return out

    def whole(j):
        return (pl.multiple_of(j * tk, tk), tk)

    def halves(j):
        return [(pl.multiple_of(j * tk + u * (tk // 2), tk // 2), tk // 2) for u in range(2)]

    def merge(carry, tiles):
        out = []
        for h in range(n_sub):
            m, l, acc = carry[h]
            m_new = m
            for t in tiles:
                m_new = jnp.maximum(m_new, t[h][0])
            a = jnp.exp2(m - m_new)
            l, acc = l * a, acc * a
            for t in tiles:
                b = jnp.exp2(t[h][0] - m_new)
                l, acc = l + t[h][1] * b, acc + t[h][2] * b
            out.append((m_new, l, acc))
        return tuple(out)

    def group_step(jj, carry):
        return merge(carry, tile_stats([whole(group * jj + u) for u in range(group)], False))

    def single_step(j, carry):
        return merge(carry, tile_stats(halves(j), False))

    def masked_step(j, carry):
        return merge(carry, tile_stats(halves(j), True))

    init = tuple((jnp.full((1, sub), NEG_BIG, jnp.float32), jnp.zeros((1, sub), jnp.float32),
                  jnp.zeros((V_DIM, sub), jnp.float32)) for _ in range(n_sub))
    n_groups = lax.shift_right_logical(nfull_ref[i], group_shift)
    carry = lax.fori_loop(0, n_groups, group_step, init)
    carry = lax.fori_loop(group * n_groups, nfull_ref[i], single_step, carry)
    carry = lax.fori_loop(nfull_ref[i], nkv_ref[i], masked_step, carry)
    for h in range(n_sub):
        _, l, acc = carry[h]
        o_ref[h * sub:(h + 1) * sub, :] = _bf16((acc / l).T)


def _attention(q, k, vt, pos_col, pos_row, tq=512, tk=512, n_sub=2, group_shift=ATTN_GROUP_SHIFT):
    heads, s, _ = q.shape
    tq = min(tq, s)
    tk = min(tk, s)
    nfull, nkv = _kv_tile_ranges(pos_row.reshape(-1), tq, tk)
    return pl.pallas_call(
        functools.partial(_attn_kernel, tk=tk, n_sub=n_sub, group_shift=group_shift),
        out_shape=jax.ShapeDtypeStruct((s, heads * V_DIM), jnp.bfloat16),
        grid_spec=pltpu.PrefetchScalarGridSpec(
            num_scalar_prefetch=2,
            grid=(heads, s // tq),
            in_specs=[pl.BlockSpec((None, tq, QK_DIM), lambda h, i, nf, nk: (h, i, 0)),
                      pl.BlockSpec((None, s, QK_DIM), lambda h, i, nf, nk: (h, 0, 0)),
                      pl.BlockSpec((V_DIM, s), lambda h, i, nf, nk: (h, 0)),
                      pl.BlockSpec((s, 1), lambda h, i, nf, nk: (0, 0)),
                      pl.BlockSpec((1, tq), lambda h, i, nf, nk: (0, i))],
            out_specs=pl.BlockSpec((tq, V_DIM), lambda h, i, nf, nk: (i, h))),
        compiler_params=_cparams(("parallel", "arbitrary")),
        name="mla_attention",
    )(nfull, nkv, q, k, vt, pos_col, pos_row)


def _kv_tile_ranges(positions, tq, tk):
    cid = positions >> CHUNK_SHIFT
    qc = cid.reshape(-1, tq)
    kc = cid.reshape(-1, tk)
    qmin, qmax = jnp.min(qc, axis=1), jnp.max(qc, axis=1)
    kmin, kmax = jnp.min(kc, axis=1), jnp.max(kc, axis=1)
    vis = kmin[None, :] <= qmax[:, None]
    last = jnp.max(jnp.where(vis, jnp.arange(kmin.shape[0], dtype=jnp.int32)[None, :], -1), axis=1)
    full = (kmax[None, :] <= qmin[:, None]).astype(jnp.int32)
    nfull = jnp.sum(jnp.cumprod(full, axis=1), axis=1)
    return nfull.astype(jnp.int32), (last + 1).astype(jnp.int32)


def _shared_kv(hk, cos, sin, w_dkv, kv_a_norm_g, w_kr, w_uk, w_uv, k_norm_g):
    ckv = _mm(hk, _bf16(w_dkv), out_dtype=jnp.bfloat16, tm=1024, tn=w_dkv.shape[1],
              mode="rmsnorm", extra=kv_a_norm_g, name="mla_ckv")
    w_kr_pad = jnp.pad(_bf16(w_kr), ((0, 0), (0, LANES - ROPE_DIM)))
    kr_raw = _mm(hk, w_kr_pad, out_dtype=jnp.float32, tm=1024, tn=LANES, name="mla_krope")
    k = _kbuild(ckv, w_uk, kr_raw, k_norm_g, cos, sin)
    vt = _mm(ckv, _bf16(w_uv), out_dtype=jnp.bfloat16, tm=1024, tn=1024, mode="transpose",
             name="mla_v")
    return k, vt


def _mla_layer(x, h, k, vt, cos, sin, pos_col, pos_row, w_dq_all, q_a_norm_g, w_uq, q_norm_g, w_o_all, layer):
    qa = _mm(h, w_dq_all, layer=layer, out_dtype=jnp.bfloat16, tm=1024, tn=w_dq_all.shape[2], tk=2048,
             mode="rmsnorm", extra=q_a_norm_g, name="mla_qa")
    q = _qbuild(qa, w_uq, q_norm_g, cos, sin)
    o = _attention(q, k, vt, pos_col, pos_row)
    return _mm(o, w_o_all, layer=layer, out_dtype=jnp.float32, tm=1024, tn=512, mode="residual", extra=x,
               name="mla_out_proj")


def kernel(x, positions, g_mix, g_ffn, ssm_w_in, ssm_conv_w, ssm_conv_b, ssm_dt_bias, ssm_a_log, ssm_d, ssm_norm_g, ssm_w_out, kv_norm_g, w_dkv, kv_a_norm_g, w_kr, w_uk, w_uv, k_norm_g, q_w_dq, q_a_norm_g, q_w_uq, q_norm_g, attn_w_o, moe_w_group, moe_w_expert, moe_w_gate, moe_w_up, moe_w_down):
    bsz, seq, d = x.shape
    assert bsz == 1
    depth = g_mix.shape[0]
    n_a = ssm_w_in.shape[0]
    xs = x.reshape(seq, d)
    pos = positions.reshape(seq).astype(jnp.int32)
    pos_col = pos.reshape(seq, 1)
    pos_row = pos.reshape(1, seq)
    cos, sin = _rope_tables(pos_col)
    k_shared = vt_shared = None
    h = _prenorm(xs, g_mix[0])
    for layer in range(depth):
        if layer < n_a:
            i = layer
            xs = _mamba_layer(xs, h, ssm_w_in, ssm_conv_w[i], ssm_conv_b[i],
                              ssm_dt_bias[i], ssm_a_log[i], ssm_d[i], ssm_norm_g[i], ssm_w_out, i)
        else:
            j = layer - n_a
            xs = _mla_layer(xs, h, k_shared, vt_shared, cos, sin, pos_col, pos_row,
                            q_w_dq, q_a_norm_g[j], q_w_uq[j], q_norm_g[j], attn_w_o, j)
        next_gains = [g_mix[layer + 1]] if layer + 1 < depth else []
        if layer == n_a - 1:
            next_gains = next_gains + [kv_norm_g]
        xs, normed = _moe_layer(xs, g_ffn[layer], moe_w_group[layer], moe_w_expert[layer],
                                moe_w_gate, moe_w_up, moe_w_down, layer, next_gains)
        if layer + 1 < depth:
            h = normed[0]
        if layer == n_a - 1:
            k_shared, vt_shared = _shared_kv(normed[-1], cos, sin, w_dkv, kv_a_norm_g, w_kr,
                                             w_uk, w_uv, k_norm_g)
    return xs.reshape(bsz, seq, d)
```

```python
import functools

import jax
import jax.numpy as jnp
from jax import lax
from jax.experimental import pallas as pl
from jax.experimental.pallas import tpu as pltpu

EPS = 1e-6
CHUNK = 64
CHUNK_SHIFT = 6

SSM_HEAD_DIM = 64
SSM_GROUPS = 8
SSM_STATE = 128
CONV_WIDTH = 4
SSD_L = 128
CONV_PAD = 8
SSD_GROUPS_PER_STEP = 4

NOPE_DIM = 128
ROPE_DIM = 64
QK_DIM = NOPE_DIM + ROPE_DIM
V_DIM = 128
ROPE_THETA = 10000.0

N_GROUPS = 4
EXPERTS_PER_GROUP = 8
N_EXPERTS = N_GROUPS * EXPERTS_PER_GROUP
TOP_K = 2
MOE_TM = 256

LANES = 128
VMEM_LIMIT_BYTES = 56 * 1024 * 1024
NEG_BIG = -1e30
LOG2E = 1.4426950408889634
Q_SCALE = QK_DIM ** -0.5 * LOG2E
HEADS_PER_STEP = 4
ATTN_GROUP_SHIFT = 2


def _cparams(sem):
    return pltpu.CompilerParams(dimension_semantics=sem, vmem_limit_bytes=VMEM_LIMIT_BYTES)


def _silu(v):
    hv = 0.5 * v
    return hv + hv * jnp.tanh(hv)


def _pack_bf16_pairs(v):
    c = v.shape[1] // 2
    bits = lax.bitcast_convert_type(_bf16(v).astype(jnp.float32), jnp.uint32)
    return bits[:, :c] | (bits[:, c:] >> 16)


def _unpack_bf16_pairs(u):
    hi = lax.bitcast_convert_type(u & jnp.uint32(0xFFFF0000), jnp.float32)
    lo = lax.bitcast_convert_type(u << 16, jnp.float32)
    return hi, lo


def _softplus(v):
    return jnp.maximum(v, 0.0) + jnp.log1p(jnp.exp(-jnp.abs(v)))


def _bf16(v):
    return v.astype(jnp.bfloat16)


def _dot(a, b):
    return jnp.dot(a, b, preferred_element_type=jnp.float32)


def _split3(v):
    hi = _bf16(v)
    r1 = v - hi.astype(jnp.float32)
    mid = _bf16(r1)
    lo = _bf16(r1 - mid.astype(jnp.float32))
    return hi, mid, lo


def _prenorm_kernel(x_ref, g_ref, o_ref):
    x = x_ref[...]
    y = x * lax.rsqrt(jnp.mean(x * x, axis=-1, keepdims=True) + EPS)
    o_ref[...] = _bf16(y * g_ref[...])


def _prenorm(x, g, tm=256):
    s, d = x.shape
    return pl.pallas_call(
        _prenorm_kernel,
        out_shape=jax.ShapeDtypeStruct((s, d), jnp.bfloat16),
        grid=(s // tm,),
        in_specs=[pl.BlockSpec((tm, d), lambda i: (i, 0)),
                  pl.BlockSpec((1, d), lambda i: (0, 0))],
        out_specs=pl.BlockSpec((tm, d), lambda i: (i, 0)),
        compiler_params=_cparams(("parallel",)),
        name="prenorm",
    )(x, g.reshape(1, d))


def _mm_kernel(*refs, nk, mode):
    a_ref, w_ref = refs[0], refs[1]
    if mode in ("plain", "transpose"):
        extra, o_ref, rest = None, refs[2], refs[3:]
    else:
        extra, o_ref, rest = refs[2], refs[3], refs[4:]

    def finish(acc):
        if mode == "transpose":
            acc = acc.T
        elif mode == "residual":
            acc = acc + extra[...]
        elif mode == "rmsnorm":
            acc = acc * lax.rsqrt(jnp.mean(acc * acc, axis=-1, keepdims=True) + EPS) * extra[...]
        o_ref[...] = acc.astype(o_ref.dtype)

    def product():
        return _dot(a_ref[...], _bf16(w_ref[...]))

    if nk == 1 and mode == "transpose":
        rest[0][...] = product()
        finish(rest[0][...])
        return
    if nk == 1:
        finish(product())
        return

    acc_ref = rest[0]
    k = pl.program_id(2)

    @pl.when(k == 0)
    def _():
        acc_ref[...] = jnp.zeros_like(acc_ref)

    acc_ref[...] += product()

    @pl.when(k == nk - 1)
    def _():
        finish(acc_ref[...])


def _mm(a, w, *, out_dtype, tm, tn, tk=None, mode="plain", extra=None, name="mm", layer=None,
        col0=0, n=None):
    m, kdim = a.shape
    n = w.shape[-1] - col0 if n is None else n
    tm = min(tm, m)
    tk = kdim if tk is None else tk
    nk = kdim // tk
    cb0 = col0 // tn
    assert col0 % tn == 0 and n % tn == 0 and kdim % tk == 0
    if layer is None:
        w_spec = pl.BlockSpec((tk, tn), lambda i, j, k: (k, j + cb0))
    else:
        w_spec = pl.BlockSpec((None, tk, tn), lambda i, j, k: (layer, k, j + cb0))
    in_specs = [pl.BlockSpec((tm, tk), lambda i, j, k: (i, k)), w_spec]
    args = [a, w]
    if mode == "residual":
        in_specs.append(pl.BlockSpec((tm, tn), lambda i, j, k: (i, j)))
        args.append(extra)
    elif mode == "rmsnorm":
        assert tn == n
        in_specs.append(pl.BlockSpec((1, tn), lambda i, j, k: (0, j)))
        args.append(extra.reshape(1, n))
    scratch = [pltpu.VMEM((tm, tn), jnp.float32)] if (nk > 1 or mode == "transpose") else []
    if mode == "transpose":
        out_shape = jax.ShapeDtypeStruct((n, m), out_dtype)
        out_spec = pl.BlockSpec((tn, tm), lambda i, j, k: (j, i))
    else:
        out_shape = jax.ShapeDtypeStruct((m, n), out_dtype)
        out_spec = pl.BlockSpec((tm, tn), lambda i, j, k: (i, j))
    return pl.pallas_call(
        functools.partial(_mm_kernel, nk=nk, mode=mode),
        out_shape=out_shape,
        grid=(m // tm, n // tn, nk),
        in_specs=in_specs,
        out_specs=out_spec,
        scratch_shapes=scratch,
        compiler_params=_cparams(("parallel", "parallel", "arbitrary")),
        name=name,
    )(*args)


def _pair_expand(cols, j, lane_lo):
    return jnp.where(lane_lo, cols[:, 2 * j:2 * j + 1], cols[:, 2 * j + 1:2 * j + 2])


def _ssd_kernel(z_ref, x_ref, b_ref, c_ref, cwx_ref, cwb_ref, cwc_ref, cbx_ref, cbb_ref, cbc_ref,
                dtc_ref, dtr_ref, biasc_ref, biasr_ref, alogc_ref, alogr_ref, dskip_ref, ng_ref,
                o_ref, xbuf, bbuf, cbuf, state):
    gw = x_ref.shape[1] // SSD_GROUPS_PER_STEP
    n = b_ref.shape[1] // SSD_GROUPS_PER_STEP
    for g in range(SSD_GROUPS_PER_STEP):
        wide = (slice(None), slice(g * gw, (g + 1) * gw))
        narrow = (slice(None), slice(g * n, (g + 1) * n))
        _ssd_group(z_ref.at[wide], x_ref.at[wide], b_ref.at[narrow], c_ref.at[narrow],
                   cwx_ref.at[wide], cwb_ref.at[narrow], cwc_ref.at[narrow],
                   cbx_ref.at[wide], cbb_ref.at[narrow], cbc_ref.at[narrow],
                   dtc_ref.at[g], dtr_ref.at[g], biasc_ref.at[g], biasr_ref.at[g], alogc_ref.at[g],
                   alogr_ref.at[g], dskip_ref.at[g], ng_ref.at[wide], o_ref.at[wide],
                   xbuf.at[g], bbuf.at[g], cbuf.at[g], state.at[g])


def _ssd_group(z_ref, x_ref, b_ref, c_ref, cwx_ref, cwb_ref, cwc_ref, cbx_ref, cbb_ref, cbc_ref,
               dtc_ref, dtr_ref, biasc_ref, biasr_ref, alogc_ref, alogr_ref, dskip_ref, ng_ref,
               o_ref, xbuf, bbuf, cbuf, state):
    L = SSD_L
    c = pl.program_id(1)

    @pl.when(c == 0)
    def _():
        xbuf[0:CONV_PAD, :] = jnp.zeros((CONV_PAD, xbuf.shape[1]), jnp.float32)
        bbuf[0:CONV_PAD, :] = jnp.zeros((CONV_PAD, bbuf.shape[1]), jnp.float32)
        cbuf[0:CONV_PAD, :] = jnp.zeros((CONV_PAD, cbuf.shape[1]), jnp.float32)
        state[...] = jnp.zeros_like(state)

    def conv_silu(src_ref, buf, w_ref, bias_ref):
        buf[CONV_PAD:CONV_PAD + L, :] = src_ref[...].astype(jnp.float32)
        acc = bias_ref[...] + w_ref[CONV_WIDTH - 1:CONV_WIDTH, :] * buf[CONV_PAD:CONV_PAD + L, :]
        for k in range(CONV_WIDTH - 1):
            off = CONV_PAD - (CONV_WIDTH - 1) + k
            acc = acc + w_ref[k:k + 1, :] * buf[off:off + L, :]
        buf[0:CONV_PAD, :] = buf[L:L + CONV_PAD, :]
        return _silu(acc)

    xs = conv_silu(x_ref, xbuf, cwx_ref, cbx_ref)
    bs = conv_silu(b_ref, bbuf, cwb_ref, cbb_ref)
    cs = conv_silu(c_ref, cbuf, cwc_ref, cbc_ref)
    bs16 = _bf16(bs)
    cs16 = _bf16(cs)

    dtc = _softplus(dtc_ref[...] + biasc_ref[...])
    dtr = _softplus(dtr_ref[...] + biasr_ref[...])
    a_c = dtc * (-jnp.exp(alogc_ref[...]))
    a_r = dtr * (-jnp.exp(alogr_ref[...]))

    rid = lax.broadcasted_iota(jnp.int32, (L, L), 0)
    cid = lax.broadcasted_iota(jnp.int32, (L, L), 1)
    causal = rid >= cid
    tril = _bf16(jnp.where(causal, 1.0, 0.0))
    triu = _bf16(jnp.where(rid <= cid, 1.0, 0.0))
    acum_c = sum(_dot(tril, t) for t in _split3(a_c))
    acum_r = sum(_dot(t, triu) for t in _split3(a_r))

    total_c = acum_c[L - 1:L, :]
    e_c = jnp.exp(acum_c)
    w_c = dtc * jnp.exp(total_c - acum_c)
    etot = jnp.exp(total_c)

    cb = lax.dot_general(cs16, bs16, (((1,), (1,)), ((), ())),
                         preferred_element_type=jnp.float32)
    st = state[...]
    yoff = _dot(cs16, _bf16(st))

    lane = lax.broadcasted_iota(jnp.int32, (L, LANES), 1)
    lane_lo = lane < SSM_HEAD_DIM
    lane1_lo = lax.broadcasted_iota(jnp.int32, (1, LANES), 1) < SSM_HEAD_DIM

    acum_c2 = acum_c * LOG2E
    row_exp2 = acum_r * LOG2E - jnp.log2(dtr)

    z = z_ref[...].astype(jnp.float32)
    n_pairs = xs.shape[1] // LANES
    ys = []
    xws = []
    for j in range(n_pairs):
        xp = xs[:, j * LANES:(j + 1) * LANES]
        ms = []
        for hh in (2 * j, 2 * j + 1):
            seg = acum_c2[:, hh:hh + 1] - row_exp2[hh:hh + 1, :]
            ms.append(_bf16(cb * jnp.where(causal, jnp.exp2(seg), 0.0)))
        mpair = jnp.concatenate(ms, axis=1)
        rhs = jnp.concatenate([jnp.where(lane_lo, xp, 0.0), jnp.where(lane_lo, 0.0, xp)], axis=0)
        ydiag = _dot(mpair, _bf16(rhs))
        y = ydiag + yoff[:, j * LANES:(j + 1) * LANES] * _pair_expand(e_c, j, lane_lo)
        y = y + xp * dskip_ref[:, j * LANES:(j + 1) * LANES]
        ys.append(y)
        xws.append(_bf16(xp * _pair_expand(w_c, j, lane_lo)))
    y = jnp.concatenate(ys, axis=1)
    xw = jnp.concatenate(xws, axis=1)

    snew = lax.dot_general(bs16, xw, (((0,), (0,)), ((), ())),
                           preferred_element_type=jnp.float32)
    dec_row = jnp.concatenate([_pair_expand(etot, j, lane1_lo) for j in range(n_pairs)], axis=1)
    state[...] = st * dec_row + snew

    y = y * _silu(z)
    y = y * lax.rsqrt(jnp.mean(y * y, axis=-1, keepdims=True) + EPS)
    o_ref[...] = _bf16(y * ng_ref[...])


def _ssd(proj, dt_raw, conv_w, conv_b, dt_bias, a_log, d_skip, norm_g):
    s = proj.shape[0]
    g = SSM_GROUPS
    heads = dt_raw.shape[1]
    hg = heads // g
    gw = hg * SSM_HEAD_DIM
    d_inner = g * gw
    n = SSM_STATE
    L = SSD_L
    xb = d_inner // gw
    bb = 2 * d_inner // n
    cb_ = bb + g
    cwb = d_inner // n
    cwc = cwb + g

    dtc = dt_raw.reshape(s, g, hg).transpose(1, 0, 2)
    dtr = dt_raw.reshape(s, g, hg).transpose(1, 2, 0)
    cw = conv_w.reshape(CONV_WIDTH, -1)
    cbias = conv_b.reshape(1, -1)
    dsk = jnp.repeat(d_skip.reshape(g, 1, hg), SSM_HEAD_DIM, axis=2)

    def gspec(shape, fn):
        return pl.BlockSpec(shape, fn)

    gs = SSD_GROUPS_PER_STEP
    assert g % gs == 0 and xb % gs == 0 and bb % gs == 0 and cb_ % gs == 0 and cwb % gs == 0 and cwc % gs == 0
    in_specs = [
        gspec((L, gs * gw), lambda gi, c: (c, gi)),
        gspec((L, gs * gw), lambda gi, c: (c, xb // gs + gi)),
        gspec((L, gs * n), lambda gi, c: (c, bb // gs + gi)),
        gspec((L, gs * n), lambda gi, c: (c, cb_ // gs + gi)),
        gspec((CONV_WIDTH, gs * gw), lambda gi, c: (0, gi)),
        gspec((CONV_WIDTH, gs * n), lambda gi, c: (0, cwb // gs + gi)),
        gspec((CONV_WIDTH, gs * n), lambda gi, c: (0, cwc // gs + gi)),
        gspec((1, gs * gw), lambda gi, c: (0, gi)),
        gspec((1, gs * n), lambda gi, c: (0, cwb // gs + gi)),
        gspec((1, gs * n), lambda gi, c: (0, cwc // gs + gi)),
        gspec((gs, L, hg), lambda gi, c: (gi, c, 0)),
        gspec((gs, hg, L), lambda gi, c: (gi, 0, c)),
        gspec((gs, 1, hg), lambda gi, c: (gi, 0, 0)),
        gspec((gs, hg, 1), lambda gi, c: (gi, 0, 0)),
        gspec((gs, 1, hg), lambda gi, c: (gi, 0, 0)),
        gspec((gs, hg, 1), lambda gi, c: (gi, 0, 0)),
        gspec((gs, 1, gw), lambda gi, c: (gi, 0, 0)),
        gspec((1, gs * gw), lambda gi, c: (0, gi)),
    ]
    return pl.pallas_call(
        _ssd_kernel,
        out_shape=jax.ShapeDtypeStruct((s, d_inner), jnp.bfloat16),
        grid=(g // gs, s // L),
        in_specs=in_specs,
        out_specs=pl.BlockSpec((L, gs * gw), lambda gi, c: (c, gi)),
        scratch_shapes=[pltpu.VMEM((gs, CONV_PAD + L, gw), jnp.float32),
                        pltpu.VMEM((gs, CONV_PAD + L, n), jnp.float32),
                        pltpu.VMEM((gs, CONV_PAD + L, n), jnp.float32),
                        pltpu.VMEM((gs, n, gw), jnp.float32)],
        compiler_params=_cparams(("parallel", "arbitrary")),
        name="ssd",
    )(proj, proj, proj, proj, cw, cw, cw, cbias, cbias, cbias, dtc, dtr,
      dt_bias.reshape(g, 1, hg), dt_bias.reshape(g, hg, 1),
      a_log.reshape(g, 1, hg), a_log.reshape(g, hg, 1), dsk, norm_g.reshape(1, d_inner))


def _mamba_layer(x, h, w_in_all, conv_w, conv_b, dt_bias, a_log, d_skip, norm_g, w_out_all, layer):
    d_inner = w_out_all.shape[1]
    n_main = 2 * d_inner + 2 * SSM_GROUPS * SSM_STATE
    n_dt = w_in_all.shape[2] - n_main
    proj = _mm(h, w_in_all, layer=layer, n=n_main, out_dtype=jnp.bfloat16, tm=1024, tn=512,
               name="ssm_in_proj")
    dt_raw = _mm(h, w_in_all, layer=layer, col0=n_main, n=n_dt, out_dtype=jnp.float32, tm=1024, tn=n_dt,
                 name="ssm_dt_proj")
    y = _ssd(proj, dt_raw, conv_w, conv_b, dt_bias, a_log, d_skip, norm_g)
    return _mm(y, w_out_all, layer=layer, out_dtype=jnp.float32, tm=1024, tn=1024, tk=2048,
               mode="residual", extra=x, name="ssm_out_proj")


def _router_kernel(x_ref, g_ref, whi_ref, wlo_ref, oi_ref, of_ref, cnt_ref, hp_ref, carry):
    i = pl.program_id(0)
    tm = x_ref.shape[0]

    @pl.when(i == 0)
    def _():
        carry[...] = jnp.zeros_like(carry)

    x = x_ref[...]
    h = x * lax.rsqrt(jnp.mean(x * x, axis=-1, keepdims=True) + EPS) * g_ref[...]
    hp_ref[...] = _pack_bf16_pairs(h)
    hhi = _bf16(h)
    hlo = _bf16(h - hhi.astype(jnp.float32))
    whi = whi_ref[...]
    logits = _dot(hhi, whi) + (_dot(hlo, whi) + _dot(hhi, wlo_ref[...]))

    lane = lax.broadcasted_iota(jnp.int32, (tm, LANES), 1)
    ninf = -jnp.inf
    gmask = lane < N_GROUPS
    gl = jnp.where(gmask, logits, ninf)
    gmax = jnp.max(gl, axis=-1, keepdims=True)
    gsel = jnp.min(jnp.where(gl == gmax, lane, LANES), axis=-1, keepdims=True)
    psel = 1.0 / jnp.sum(jnp.where(gmask, jnp.exp(logits - gmax), 0.0), axis=-1, keepdims=True)
    lo = N_GROUPS + EXPERTS_PER_GROUP * gsel
    emask = (lane >= lo) & (lane < lo + EXPERTS_PER_GROUP)
    el = jnp.where(emask, logits, ninf)
    v1 = jnp.max(el, axis=-1, keepdims=True)
    i1 = jnp.min(jnp.where(el == v1, lane, LANES), axis=-1, keepdims=True)
    el2 = jnp.where(lane == i1, ninf, el)
    v2 = jnp.max(el2, axis=-1, keepdims=True)
    i2 = jnp.min(jnp.where(el2 == v2, lane, LANES), axis=-1, keepdims=True)
    e21 = jnp.exp(v2 - v1)
    w1 = psel / (1.0 + e21)
    w2 = psel * e21 / (1.0 + e21)

    hit1 = lane == i1
    hit2 = lane == i2
    onehot = jnp.where(hit1 | hit2, 1.0, 0.0)
    rid = lax.broadcasted_iota(jnp.int32, (tm, tm), 0)
    cid = lax.broadcasted_iota(jnp.int32, (tm, tm), 1)
    strict = _bf16(jnp.where(rid > cid, 1.0, 0.0))
    before = _dot(strict, _bf16(onehot)) + carry[...]
    r1 = jnp.sum(jnp.where(hit1, before, 0.0), axis=-1, keepdims=True)
    r2 = jnp.sum(jnp.where(hit2, before, 0.0), axis=-1, keepdims=True)
    carry[...] += jnp.sum(onehot, axis=0, keepdims=True)

    e1 = i1 - N_GROUPS
    e2 = i2 - N_GROUPS
    oi = jnp.where(lane == 0, e1, jnp.where(lane == 1, e2, jnp.where(
        lane == 2, r1.astype(jnp.int32), jnp.where(lane == 3, r2.astype(jnp.int32), 0))))
    oi_ref[...] = oi
    of_ref[...] = jnp.where(lane == 0, w1, jnp.where(lane == 1, w2, 0.0))
    cnt_ref[...] = carry[...]


def _router(x, g_ffn, w_group, w_expert, tm=256):
    s, d = x.shape
    wr = jnp.concatenate([w_group, w_expert], axis=1)
    wr = jnp.pad(wr, ((0, 0), (0, LANES - wr.shape[1])))
    whi = _bf16(wr)
    wlo = _bf16(wr - whi.astype(jnp.float32))
    return pl.pallas_call(
        _router_kernel,
        out_shape=(jax.ShapeDtypeStruct((s, LANES), jnp.int32),
                   jax.ShapeDtypeStruct((s, LANES), jnp.float32),
                   jax.ShapeDtypeStruct((1, LANES), jnp.float32),
                   jax.ShapeDtypeStruct((s, d // 2), jnp.uint32)),
        grid=(s // tm,),
        in_specs=[pl.BlockSpec((tm, d), lambda i: (i, 0)),
                  pl.BlockSpec((1, d), lambda i: (0, 0)),
                  pl.BlockSpec((d, LANES), lambda i: (0, 0)),
                  pl.BlockSpec((d, LANES), lambda i: (0, 0))],
        out_specs=(pl.BlockSpec((tm, LANES), lambda i: (i, 0)),
                   pl.BlockSpec((tm, LANES), lambda i: (i, 0)),
                   pl.BlockSpec((1, LANES), lambda i: (0, 0)),
                   pl.BlockSpec((tm, d // 2), lambda i: (i, 0))),
        scratch_shapes=[pltpu.VMEM((1, LANES), jnp.float32)],
        compiler_params=_cparams(("arbitrary",)),
        name="moe_router",
    )(x, g_ffn.reshape(1, d), whi, wlo)


def _invert_kernel(pos_ref, src_ref):
    def clear(r, carry):
        src_ref[r] = 0
        return carry

    def place(a, carry):
        src_ref[pos_ref[a]] = lax.shift_right_logical(a, 1)
        return carry

    lax.fori_loop(0, src_ref.shape[0], clear, 0, unroll=8)
    lax.fori_loop(0, pos_ref.shape[0], place, 0, unroll=8)


def _invert(pos_flat, n_rows):
    assert TOP_K == 2
    return pl.pallas_call(
        _invert_kernel,
        out_shape=jax.ShapeDtypeStruct((n_rows,), jnp.int32),
        in_specs=[pl.BlockSpec(memory_space=pltpu.SMEM)],
        out_specs=pl.BlockSpec(memory_space=pltpu.SMEM),
        name="moe_invert",
    )(pos_flat)


def _expert_kernel(te_ref, tv_ref, first_ref, nexte_ref, wslot_ref, src_ref, x_hbm, wg_hbm, wu_hbm, wd_hbm,
                   ys_ref, xbuf, wbuf_g, wbuf_u, wbuf_d, wg16, wu16, wd16, sem, wsem, *, layer):
    i = pl.program_id(0)
    nt = pl.num_programs(0)
    tm = xbuf.shape[1]

    def gather(tile, slot):
        def issue(r, carry):
            tok = src_ref[tile * tm + r]
            pltpu.make_async_copy(x_hbm.at[pl.ds(tok, 1)], xbuf.at[slot, pl.ds(r, 1)],
                                  sem.at[slot]).start()
            return carry
        lax.fori_loop(0, tm, issue, 0, unroll=8)

    def weight_copies(e, slot):
        return (pltpu.make_async_copy(wg_hbm.at[layer, e], wbuf_g.at[slot], wsem.at[slot, 0]),
                pltpu.make_async_copy(wu_hbm.at[layer, e], wbuf_u.at[slot], wsem.at[slot, 1]),
                pltpu.make_async_copy(wd_hbm.at[layer, e], wbuf_d.at[slot], wsem.at[slot, 2]))

    @pl.when(i == 0)
    def _():
        gather(0, 0)
        for c in weight_copies(te_ref[0], 0):
            c.start()

    nxt = jnp.minimum(i + 1, nt - 1)

    @pl.when((i + 1 < nt) & (tv_ref[nxt] > 0))
    def _():
        gather(i + 1, (i + 1) % 2)

    @pl.when(first_ref[i] > 0)
    def _():
        ws = wslot_ref[i]
        for c in weight_copies(te_ref[i], ws):
            c.wait()
        ne = nexte_ref[i]

        @pl.when(ne >= 0)
        def _():
            for c in weight_copies(ne, 1 - ws):
                c.start()

        wg16[...] = _bf16(wbuf_g[ws])
        wu16[...] = _bf16(wbuf_u[ws])
        wd16[...] = _bf16(wbuf_d[ws])

    @pl.when(tv_ref[i] > 0)
    def _():
        slot = i % 2
        pltpu.make_async_copy(x_hbm.at[pl.ds(0, tm)], xbuf.at[slot], sem.at[slot]).wait()
        half = xbuf.shape[2]
        h_hi, h_lo = (_bf16(t) for t in _unpack_bf16_pairs(xbuf[slot]))
        gate = _dot(h_hi, wg16[0:half, :]) + _dot(h_lo, wg16[half:, :])
        up = _dot(h_hi, wu16[0:half, :]) + _dot(h_lo, wu16[half:, :])
        hid = _bf16(_silu(gate) * up)
        ys_ref[...] = _pack_bf16_pairs(_dot(hid, wd16[...]))

    @pl.when(tv_ref[i] == 0)
    def _():
        ys_ref[...] = jnp.zeros_like(ys_ref)


def _experts(hp, src, w_gate, w_up, w_down, layer, tile_e, tile_v, counts):
    dh = hp.shape[1]
    d = 2 * dh
    ff = w_gate.shape[-1]
    n_rows = src.shape[0]
    nt = n_rows // MOE_TM
    idx = jnp.arange(nt, dtype=jnp.int32)
    prev_e = jnp.concatenate([jnp.full((1,), -1, jnp.int32), tile_e[:-1]])
    first = ((tile_v > 0) & ((idx == 0) | (tile_e != prev_e))).astype(jnp.int32)
    wslot = (jnp.cumsum(first) - 1) & 1
    eids = jnp.arange(N_EXPERTS, dtype=jnp.int32)
    present = jnp.where(counts > 0, eids, N_EXPERTS)
    at_or_after = lax.cummin(present, reverse=True)
    after = jnp.concatenate([at_or_after[1:], jnp.full((1,), N_EXPERTS, jnp.int32)])
    after = jnp.where(after >= N_EXPERTS, -1, after)
    nexte = jnp.where(first > 0, after[tile_e], -1).astype(jnp.int32)
    any_spec = pl.BlockSpec(memory_space=pl.ANY)
    return pl.pallas_call(
        functools.partial(_expert_kernel, layer=layer),
        out_shape=jax.ShapeDtypeStruct((n_rows, dh), jnp.uint32),
        grid_spec=pltpu.PrefetchScalarGridSpec(
            num_scalar_prefetch=6,
            grid=(nt,),
            in_specs=[any_spec, any_spec, any_spec, any_spec],
            out_specs=pl.BlockSpec((MOE_TM, dh), lambda i, *_: (i, 0)),
            scratch_shapes=[pltpu.VMEM((2, MOE_TM, dh), jnp.uint32),
                            pltpu.VMEM((2, d, ff), jnp.float32),
                            pltpu.VMEM((2, d, ff), jnp.float32),
                            pltpu.VMEM((2, ff, d), jnp.float32),
                            pltpu.VMEM((d, ff), jnp.bfloat16),
                            pltpu.VMEM((d, ff), jnp.bfloat16),
                            pltpu.VMEM((ff, d), jnp.bfloat16),
                            pltpu.SemaphoreType.DMA((2,)),
                            pltpu.SemaphoreType.DMA((2, 3))]),
        compiler_params=_cparams(("arbitrary",)),
        name="moe_experts",
    )(tile_e, tile_v, first, nexte, wslot.astype(jnp.int32), src, hp, w_gate, w_up, w_down)


def _combine_kernel(pos_ref, x_ref, wt_ref, ys_hbm, *rest, tc, n_norm):
    gain_refs = rest[:n_norm]
    o_ref = rest[n_norm]
    h_refs = rest[n_norm + 1:2 * n_norm + 1]
    buf, sem = rest[2 * n_norm + 1:]
    i = pl.program_id(0)
    n = pl.num_programs(0)

    def gather(step, slot):
        def issue(t, carry):
            for k in range(TOP_K):
                p = pos_ref[TOP_K * (step * tc + t) + k]
                pltpu.make_async_copy(ys_hbm.at[pl.ds(p, 1)], buf.at[slot, k, pl.ds(t, 1)],
                                      sem.at[slot]).start()
            return carry
        lax.fori_loop(0, tc, issue, 0, unroll=8)

    @pl.when(i == 0)
    def _():
        gather(0, 0)

    @pl.when(i + 1 < n)
    def _():
        gather(i + 1, (i + 1) % 2)

    slot = i % 2
    for k in range(TOP_K):
        pltpu.make_async_copy(ys_hbm.at[pl.ds(0, tc)], buf.at[slot, k], sem.at[slot]).wait()
    half = buf.shape[3]
    rows_per_chunk = 32

    def chunk(c, carry):
        rows = pl.ds(pl.multiple_of(c * rows_per_chunk, rows_per_chunk), rows_per_chunk)
        wt = wt_ref[rows, :]
        y0_hi, y0_lo = _unpack_bf16_pairs(buf[slot, 0, rows, :])
        y1_hi, y1_lo = _unpack_bf16_pairs(buf[slot, 1, rows, :])
        out_hi = x_ref[rows, 0:half] + wt[:, 0:1] * y0_hi + wt[:, 1:2] * y1_hi
        out_lo = x_ref[rows, half:] + wt[:, 0:1] * y0_lo + wt[:, 1:2] * y1_lo
        o_ref[rows, 0:half] = out_hi
        o_ref[rows, half:] = out_lo
        if n_norm:
            ss = (jnp.sum(out_hi * out_hi, axis=-1, keepdims=True)
                  + jnp.sum(out_lo * out_lo, axis=-1, keepdims=True))
            scale = lax.rsqrt(ss * (1.0 / (2 * half)) + EPS)
            for g_ref, h_ref in zip(gain_refs, h_refs):
                h_ref[rows, 0:half] = _bf16(out_hi * scale * g_ref[:, 0:half])
                h_ref[rows, half:] = _bf16(out_lo * scale * g_ref[:, half:])
        return carry

    lax.fori_loop(0, tc // rows_per_chunk, chunk, 0)


def _combine(x, wts, ys, pos_flat, gains, tc=256):
    s, d = x.shape
    n_norm = len(gains)
    row_spec = pl.BlockSpec((tc, d), lambda i, pos: (i, 0))
    gain_spec = pl.BlockSpec((1, d), lambda i, pos: (0, 0))
    outs = pl.pallas_call(
        functools.partial(_combine_kernel, tc=tc, n_norm=n_norm),
        out_shape=(jax.ShapeDtypeStruct((s, d), jnp.float32),)
        + (jax.ShapeDtypeStruct((s, d), jnp.bfloat16),) * n_norm,
        grid_spec=pltpu.PrefetchScalarGridSpec(
            num_scalar_prefetch=1,
            grid=(s // tc,),
            in_specs=[row_spec, pl.BlockSpec((tc, LANES), lambda i, pos: (i, 0)),
                      pl.BlockSpec(memory_space=pl.ANY)] + [gain_spec] * n_norm,
            out_specs=(row_spec,) * (1 + n_norm),
            scratch_shapes=[pltpu.VMEM((2, TOP_K, tc, d // 2), jnp.uint32),
                            pltpu.SemaphoreType.DMA((2,))]),
        compiler_params=_cparams(("arbitrary",)),
        name="moe_combine",
    )(pos_flat, x, wts, ys, *[g.reshape(1, d) for g in gains])
    return outs[0], list(outs[1:])


def _moe_layer(x, g_ffn, w_group, w_expert, w_gate, w_up, w_down, layer, next_gains):
    s = x.shape[0]
    oi, wts, cnt, hp = _router(x, g_ffn, w_group, w_expert)
    e = oi[:, 0:TOP_K]
    rank = oi[:, TOP_K:2 * TOP_K]
    counts = cnt[0, N_GROUPS:N_GROUPS + N_EXPERTS].astype(jnp.int32)
    tiles = (counts + MOE_TM - 1) // MOE_TM
    tile_end = jnp.cumsum(tiles)
    row_off = (tile_end - tiles) * MOE_TM
    pos = (row_off[e] + rank).reshape(-1).astype(jnp.int32)
    nt = s * TOP_K // MOE_TM + N_EXPERTS
    tidx = jnp.minimum(jnp.arange(nt, dtype=jnp.int32), tile_end[-1] - 1)
    tile_e = jnp.minimum(jnp.searchsorted(tile_end, tidx, side="right"), N_EXPERTS - 1)
    tile_v = (jnp.arange(nt, dtype=jnp.int32) < tile_end[-1]).astype(jnp.int32)
    src = _invert(pos, nt * MOE_TM)
    ys = _experts(hp, src, w_gate, w_up, w_down, layer, tile_e.astype(jnp.int32), tile_v, counts)
    return _combine(x, wts, ys, pos, next_gains)


def _rope_table_kernel(pos_ref, inv_ref, cos_ref, sin_ref):
    ang = pos_ref[...].astype(jnp.float32) * inv_ref[...]
    cos_ref[...] = jnp.cos(ang)
    sin_ref[...] = jnp.sin(ang)


def _rope_tables(pos_col, tm=512):
    s = pos_col.shape[0]
    half = ROPE_DIM // 2
    inv = jnp.power(ROPE_THETA, -jnp.arange(0, ROPE_DIM, 2, dtype=jnp.float32) / ROPE_DIM)
    inv = jnp.concatenate([inv, inv, jnp.zeros((LANES - 2 * half,), jnp.float32)]).reshape(1, LANES)
    return pl.pallas_call(
        _rope_table_kernel,
        out_shape=(jax.ShapeDtypeStruct((s, LANES), jnp.float32),) * 2,
        grid=(s // tm,),
        in_specs=[pl.BlockSpec((tm, 1), lambda i: (i, 0)),
                  pl.BlockSpec((1, LANES), lambda i: (0, 0))],
        out_specs=(pl.BlockSpec((tm, LANES), lambda i: (i, 0)),) * 2,
        compiler_params=_cparams(("parallel",)),
        name="rope_tables",
    )(pos_col, inv)


def _rot_half_matrix():
    half = ROPE_DIM // 2
    r = jnp.arange(LANES)[:, None]
    c = jnp.arange(LANES)[None, :]
    plus = (c >= half) & (c < ROPE_DIM) & (r == c - half)
    minus = (c < half) & (r == c + half)
    return _bf16(plus.astype(jnp.float32) - minus.astype(jnp.float32))


def _head_norm_rope(o_ref, h, nope, rope128, gn, gr128, cos, sin, rot_m, out_scale):
    ones = jnp.ones((LANES, LANES), jnp.bfloat16)
    ss = _dot(_bf16(nope * nope), ones) + _dot(_bf16(rope128 * rope128), ones)
    scale = lax.rsqrt(ss * (1.0 / QK_DIM) + EPS) * out_scale
    rg = rope128 * gr128
    roped = rg * cos + _dot(_bf16(rg), rot_m) * sin
    o_ref[h, :, 0:NOPE_DIM] = _bf16(nope * gn * scale)
    o_ref[h, :, NOPE_DIM:QK_DIM] = _bf16((roped * scale)[:, :ROPE_DIM])


def _kbuild_kernel(ckv_ref, w_ref, kr_ref, gn_ref, gr_ref, cos_ref, sin_ref, rot_ref, o_ref):
    kns = [_dot(ckv_ref[...], w_ref[h]) for h in range(w_ref.shape[0])]
    for h, kn in enumerate(kns):
        _head_norm_rope(o_ref, h, kn, kr_ref[...], gn_ref[...], gr_ref[...], cos_ref[...], sin_ref[...],
                        rot_ref[...], 1.0)


def _qbuild_kernel(qa_ref, w_ref, gn_ref, gr_ref, cos_ref, sin_ref, rot_ref, o_ref):
    qs = [_dot(qa_ref[...], w_ref[h]) for h in range(w_ref.shape[0])]
    for h, q in enumerate(qs):
        _head_norm_rope(o_ref, h, q[:, :NOPE_DIM], q[:, NOPE_DIM:], gn_ref[...], gr_ref[...],
                        cos_ref[...], sin_ref[...], rot_ref[...], Q_SCALE)


def _norm_gain_split(g):
    gn = g[:NOPE_DIM].reshape(1, NOPE_DIM)
    gr = jnp.pad(g[NOPE_DIM:], (0, LANES - ROPE_DIM)).reshape(1, LANES)
    return gn, gr


def _kbuild(ckv, w_uk, kr_raw, k_norm_g, cos, sin, tm=512, hb=HEADS_PER_STEP):
    s, r = ckv.shape
    tm = min(tm, s)
    heads = w_uk.shape[1] // NOPE_DIM
    w = _bf16(w_uk).reshape(r, heads, NOPE_DIM).transpose(1, 0, 2)
    gn, gr = _norm_gain_split(k_norm_g)
    return pl.pallas_call(
        _kbuild_kernel,
        out_shape=jax.ShapeDtypeStruct((heads, s, QK_DIM), jnp.bfloat16),
        grid=(s // tm, heads // hb),
        in_specs=[pl.BlockSpec((tm, r), lambda i, h: (i, 0)),
                  pl.BlockSpec((hb, r, NOPE_DIM), lambda i, h: (h, 0, 0)),
                  pl.BlockSpec((tm, LANES), lambda i, h: (i, 0)),
                  pl.BlockSpec((1, NOPE_DIM), lambda i, h: (0, 0)),
                  pl.BlockSpec((1, LANES), lambda i, h: (0, 0)),
                  pl.BlockSpec((tm, LANES), lambda i, h: (i, 0)),
                  pl.BlockSpec((tm, LANES), lambda i, h: (i, 0)),
                  pl.BlockSpec((LANES, LANES), lambda i, h: (0, 0))],
        out_specs=pl.BlockSpec((hb, tm, QK_DIM), lambda i, h: (h, i, 0)),
        compiler_params=_cparams(("parallel", "arbitrary")),
        name="mla_k_build",
    )(ckv, w, kr_raw, gn, gr, cos, sin, _rot_half_matrix())


def _qbuild(qa, w_uq, q_norm_g, cos, sin, tm=512, hb=HEADS_PER_STEP):
    s, r = qa.shape
    tm = min(tm, s)
    heads = w_uq.shape[1] // QK_DIM
    w = _bf16(w_uq).reshape(r, heads, QK_DIM).transpose(1, 0, 2)
    w = jnp.pad(w, ((0, 0), (0, 0), (0, 2 * LANES - QK_DIM)))
    gn, gr = _norm_gain_split(q_norm_g)
    return pl.pallas_call(
        _qbuild_kernel,
        out_shape=jax.ShapeDtypeStruct((heads, s, QK_DIM), jnp.bfloat16),
        grid=(s // tm, heads // hb),
        in_specs=[pl.BlockSpec((tm, r), lambda i, h: (i, 0)),
                  pl.BlockSpec((hb, r, 2 * LANES), lambda i, h: (h, 0, 0)),
                  pl.BlockSpec((1, NOPE_DIM), lambda i, h: (0, 0)),
                  pl.BlockSpec((1, LANES), lambda i, h: (0, 0)),
                  pl.BlockSpec((tm, LANES), lambda i, h: (i, 0)),
                  pl.BlockSpec((tm, LANES), lambda i, h: (i, 0)),
                  pl.BlockSpec((LANES, LANES), lambda i, h: (0, 0))],
        out_specs=pl.BlockSpec((hb, tm, QK_DIM), lambda i, h: (h, i, 0)),
        compiler_params=_cparams(("parallel", "arbitrary")),
        name="mla_q_build",
    )(qa, w, gn, gr, cos, sin, _rot_half_matrix())


def _reduce_rows(x, pair_op, final_op):
    rows = x.shape[0]
    while rows > 8:
        rows //= 2
        x = pair_op(x[:rows], x[rows:])
    return final_op(x, axis=0, keepdims=True)


def _attn_kernel(nfull_ref, nkv_ref, q_ref, k_ref, vt_ref, kpos_ref, qpos_ref, o_ref, *, tk, n_sub,
                 group_shift):
    i = pl.program_id(1)
    group = 1 << group_shift
    sub = q_ref.shape[0] // n_sub
    qs = [q_ref[h * sub:(h + 1) * sub, :] for h in range(n_sub)]
    qcs = [qpos_ref[:, h * sub:(h + 1) * sub] >> CHUNK_SHIFT for h in range(n_sub)]

    def tile_stats(pieces, masked):
        chains = [(t, h) for t in range(len(pieces)) for h in range(n_sub)]
        scores = {}
        for t, h in chains:
            off, size = pieces[t]
            k = k_ref[pl.ds(off, size), :]
            s = lax.dot_general(k, qs[h], (((1,), (1,)), ((), ())),
                                preferred_element_type=jnp.float32)
            if masked:
                kc = kpos_ref[pl.ds(off, size), :] >> CHUNK_SHIFT
                s = jnp.where(kc <= qcs[h], s, NEG_BIG)
            scores[t, h] = s
        soft = {}
        for t, h in chains:
            m_t = _reduce_rows(scores[t, h], jnp.maximum, jnp.max)
            p = jnp.exp2(scores[t, h] - m_t)
            soft[t, h] = (m_t, _reduce_rows(p, jnp.add, jnp.sum), _bf16(p))
        out = [[None] * n_sub for _ in pieces]
        for t, h in chains:
            off, size = pieces[t]
            vt = vt_ref[:, pl.ds(off, size)]
            out[t][h] = (soft[t, h][0], soft[t, h][1], _dot(vt, soft[t, h][2]))
        return out

    def whole(j):
        return (pl.multiple_of(j * tk, tk), tk)

    def halves(j):
        return [(pl.multiple_of(j * tk + u * (tk // 2), tk // 2), tk // 2) for u in range(2)]

    def merge(carry, tiles):
        out = []
        for h in range(n_sub):
            m, l, acc = carry[h]
            m_new = m
            for t in tiles:
                m_new = jnp.maximum(m_new, t[h][0])
            a = jnp.exp2(m - m_new)
            l, acc = l * a, acc * a
            for t in tiles:
                b = jnp.exp2(t[h][0] - m_new)
                l, acc = l + t[h][1] * b, acc + t[h][2] * b
            out.append((m_new, l, acc))
        return tuple(out)

    def group_step(jj, carry):
        return merge(carry, tile_stats([whole(group * jj + u) for u in range(group)], False))

    def single_step(j, carry):
        return merge(carry, tile_stats(halves(j), False))

    def masked_step(j, carry):
        return merge(carry, tile_stats(halves(j), True))

    init = tuple((jnp.full((1, sub), NEG_BIG, jnp.float32), jnp.zeros((1, sub), jnp.float32),
                  jnp.zeros((V_DIM, sub), jnp.float32)) for _ in range(n_sub))
    def pair_step(jj, carry):
        return merge(carry, tile_stats([whole(2 * jj), whole(2 * jj + 1)], False))

    nfull = nfull_ref[i]
    n_groups = lax.shift_right_logical(nfull, group_shift)
    n_pairs = lax.shift_right_logical(nfull, 1)
    carry = lax.fori_loop(0, n_groups, group_step, init)
    carry = lax.fori_loop(n_groups * (group // 2), n_pairs, pair_step, carry)
    carry = lax.fori_loop(2 * n_pairs, nfull, single_step, carry)
    carry = lax.fori_loop(nfull, nkv_ref[i], masked_step, carry)
    for h in range(n_sub):
        _, l, acc = carry[h]
        o_ref[h * sub:(h + 1) * sub, :] = _bf16((acc / l).T)


def _attention(q, k, vt, pos_col, pos_row, tq=512, tk=512, n_sub=2, group_shift=ATTN_GROUP_SHIFT):
    heads, s, _ = q.shape
    tq = min(tq, s)
    tk = min(tk, s)
    nfull, nkv = _kv_tile_ranges(pos_row.reshape(-1), tq, tk)
    return pl.pallas_call(
        functools.partial(_attn_kernel, tk=tk, n_sub=n_sub, group_shift=group_shift),
        out_shape=jax.ShapeDtypeStruct((s, heads * V_DIM), jnp.bfloat16),
        grid_spec=pltpu.PrefetchScalarGridSpec(
            num_scalar_prefetch=2,
            grid=(heads, s // tq),
            in_specs=[pl.BlockSpec((None, tq, QK_DIM), lambda h, i, nf, nk: (h, i, 0)),
                      pl.BlockSpec((None, s, QK_DIM), lambda h, i, nf, nk: (h, 0, 0)),
                      pl.BlockSpec((V_DIM, s), lambda h, i, nf, nk: (h, 0)),
                      pl.BlockSpec((s, 1), lambda h, i, nf, nk: (0, 0)),
                      pl.BlockSpec((1, tq), lambda h, i, nf, nk: (0, i))],
            out_specs=pl.BlockSpec((tq, V_DIM), lambda h, i, nf, nk: (i, h))),
        compiler_params=_cparams(("parallel", "arbitrary")),
        name="mla_attention",
    )(nfull, nkv, q, k, vt, pos_col, pos_row)


def _kv_tile_ranges(positions, tq, tk):
    cid = positions >> CHUNK_SHIFT
    qc = cid.reshape(-1, tq)
    kc = cid.reshape(-1, tk)
    qmin, qmax = jnp.min(qc, axis=1), jnp.max(qc, axis=1)
    kmin, kmax = jnp.min(kc, axis=1), jnp.max(kc, axis=1)
    vis = kmin[None, :] <= qmax[:, None]
    last = jnp.max(jnp.where(vis, jnp.arange(kmin.shape[0], dtype=jnp.int32)[None, :], -1), axis=1)
    full = (kmax[None, :] <= qmin[:, None]).astype(jnp.int32)
    nfull = jnp.sum(jnp.cumprod(full, axis=1), axis=1)
    return nfull.astype(jnp.int32), (last + 1).astype(jnp.int32)


def _shared_kv(hk, cos, sin, w_dkv, kv_a_norm_g, w_kr, w_uk, w_uv, k_norm_g):
    ckv = _mm(hk, _bf16(w_dkv), out_dtype=jnp.bfloat16, tm=1024, tn=w_dkv.shape[1],
              mode="rmsnorm", extra=kv_a_norm_g, name="mla_ckv")
    w_kr_pad = jnp.pad(_bf16(w_kr), ((0, 0), (0, LANES - ROPE_DIM)))
    kr_raw = _mm(hk, w_kr_pad, out_dtype=jnp.float32, tm=1024, tn=LANES, name="mla_krope")
    k = _kbuild(ckv, w_uk, kr_raw, k_norm_g, cos, sin)
    vt = _mm(ckv, _bf16(w_uv), out_dtype=jnp.bfloat16, tm=1024, tn=1024, mode="transpose",
             name="mla_v")
    return k, vt


def _mla_layer(x, h, k, vt, cos, sin, pos_col, pos_row, w_dq_all, q_a_norm_g, w_uq, q_norm_g, w_o_all, layer):
    qa = _mm(h, w_dq_all, layer=layer, out_dtype=jnp.bfloat16, tm=1024, tn=w_dq_all.shape[2], tk=2048,
             mode="rmsnorm", extra=q_a_norm_g, name="mla_qa")
    q = _qbuild(qa, w_uq, q_norm_g, cos, sin)
    o = _attention(q, k, vt, pos_col, pos_row)
    return _mm(o, w_o_all, layer=layer, out_dtype=jnp.float32, tm=1024, tn=512, mode="residual", extra=x,
               name="mla_out_proj")


def kernel(x, positions, g_mix, g_ffn, ssm_w_in, ssm_conv_w, ssm_conv_b, ssm_dt_bias, ssm_a_log, ssm_d, ssm_norm_g, ssm_w_out, kv_norm_g, w_dkv, kv_a_norm_g, w_kr, w_uk, w_uv, k_norm_g, q_w_dq, q_a_norm_g, q_w_uq, q_norm_g, attn_w_o, moe_w_group, moe_w_expert, moe_w_gate, moe_w_up, moe_w_down):
    bsz, seq, d = x.shape
    assert bsz == 1
    depth = g_mix.shape[0]
    n_a = ssm_w_in.shape[0]
    xs = x.reshape(seq, d)
    pos = positions.reshape(seq).astype(jnp.int32)
    pos_col = pos.reshape(seq, 1)
    pos_row = pos.reshape(1, seq)
    cos, sin = _rope_tables(pos_col)
    k_shared = vt_shared = None
    h = _prenorm(xs, g_mix[0])
    for layer in range(depth):
        if layer < n_a:
            i = layer
            xs = _mamba_layer(xs, h, ssm_w_in, ssm_conv_w[i], ssm_conv_b[i],
                              ssm_dt_bias[i], ssm_a_log[i], ssm_d[i], ssm_norm_g[i], ssm_w_out, i)
        else:
            j = layer - n_a
            xs = _mla_layer(xs, h, k_shared, vt_shared, cos, sin, pos_col, pos_row,
                            q_w_dq, q_a_norm_g[j], q_w_uq[j], q_norm_g[j], attn_w_o, j)
        next_gains = [g_mix[layer + 1]] if layer + 1 < depth else []
        if layer == n_a - 1:
            next_gains = next_gains + [kv_norm_g]
        xs, normed = _moe_layer(xs, g_ffn[layer], moe_w_group[layer], moe_w_expert[layer],
                                moe_w_gate, moe_w_up, moe_w_down, layer, next_gains)
        if layer + 1 < depth:
            h = normed[0]
        if layer == n_a - 1:
            k_shared, vt_shared = _shared_kv(normed[-1], cos, sin, w_dkv, kv_a_norm_g, w_kr,
                                             w_uk, w_uv, k_norm_g)
    return xs.reshape(bsz, seq, d)
```

```python
import functools

import jax
import jax.numpy as jnp
from jax import lax
from jax.experimental import pallas as pl
from jax.experimental.pallas import tpu as pltpu

EPS = 1e-6
CHUNK = 64
CHUNK_SHIFT = 6

SSM_HEAD_DIM = 64
SSM_GROUPS = 8
SSM_STATE = 128
CONV_WIDTH = 4
SSD_L = 128
CONV_PAD = 8
SSD_GROUPS_PER_STEP = 4

NOPE_DIM = 128
ROPE_DIM = 64
QK_DIM = NOPE_DIM + ROPE_DIM
V_DIM = 128
ROPE_THETA = 10000.0

N_GROUPS = 4
EXPERTS_PER_GROUP = 8
N_EXPERTS = N_GROUPS * EXPERTS_PER_GROUP
TOP_K = 2
MOE_TM = 256

LANES = 128
VMEM_LIMIT_BYTES = 56 * 1024 * 1024
NEG_BIG = -1e30
LOG2E = 1.4426950408889634
Q_SCALE = QK_DIM ** -0.5 * LOG2E
HEADS_PER_STEP = 4
ATTN_GROUP_SHIFT = 2


def _cparams(sem):
    return pltpu.CompilerParams(dimension_semantics=sem, vmem_limit_bytes=VMEM_LIMIT_BYTES)


def _silu(v):
    hv = 0.5 * v
    return hv + hv * jnp.tanh(hv)


def _pack_bf16_pairs(v):
    c = v.shape[1] // 2
    bits = lax.bitcast_convert_type(_bf16(v).astype(jnp.float32), jnp.uint32)
    return bits[:, :c] | (bits[:, c:] >> 16)


def _unpack_bf16_pairs(u):
    hi = lax.bitcast_convert_type(u & jnp.uint32(0xFFFF0000), jnp.float32)
    lo = lax.bitcast_convert_type(u << 16, jnp.float32)
    return hi, lo


def _softplus(v):
    return jnp.maximum(v, 0.0) + jnp.log1p(jnp.exp(-jnp.abs(v)))


def _bf16(v):
    return v.astype(jnp.bfloat16)


def _dot(a, b):
    return jnp.dot(a, b, preferred_element_type=jnp.float32)


def _split3(v):
    hi = _bf16(v)
    r1 = v - hi.astype(jnp.float32)
    mid = _bf16(r1)
    lo = _bf16(r1 - mid.astype(jnp.float32))
    return hi, mid, lo


def _prenorm_kernel(x_ref, g_ref, o_ref):
    x = x_ref[...]
    y = x * lax.rsqrt(jnp.mean(x * x, axis=-1, keepdims=True) + EPS)
    o_ref[...] = _bf16(y * g_ref[...])


def _prenorm(x, g, tm=256):
    s, d = x.shape
    return pl.pallas_call(
        _prenorm_kernel,
        out_shape=jax.ShapeDtypeStruct((s, d), jnp.bfloat16),
        grid=(s // tm,),
        in_specs=[pl.BlockSpec((tm, d), lambda i: (i, 0)),
                  pl.BlockSpec((1, d), lambda i: (0, 0))],
        out_specs=pl.BlockSpec((tm, d), lambda i: (i, 0)),
        compiler_params=_cparams(("parallel",)),
        name="prenorm",
    )(x, g.reshape(1, d))


def _mm_kernel(*refs, nk, mode):
    a_ref, w_ref = refs[0], refs[1]
    if mode in ("plain", "transpose"):
        extra, o_ref, rest = None, refs[2], refs[3:]
    else:
        extra, o_ref, rest = refs[2], refs[3], refs[4:]

    def finish(acc):
        if mode == "transpose":
            acc = acc.T
        elif mode == "residual":
            acc = acc + extra[...]
        elif mode == "rmsnorm":
            acc = acc * lax.rsqrt(jnp.mean(acc * acc, axis=-1, keepdims=True) + EPS) * extra[...]
        o_ref[...] = acc.astype(o_ref.dtype)

    def product():
        return _dot(a_ref[...], _bf16(w_ref[...]))

    if nk == 1 and mode == "transpose":
        rest[0][...] = product()
        finish(rest[0][...])
        return
    if nk == 1:
        finish(product())
        return

    acc_ref = rest[0]
    k = pl.program_id(2)

    @pl.when(k == 0)
    def _():
        acc_ref[...] = jnp.zeros_like(acc_ref)

    acc_ref[...] += product()

    @pl.when(k == nk - 1)
    def _():
        finish(acc_ref[...])


def _mm(a, w, *, out_dtype, tm, tn, tk=None, mode="plain", extra=None, name="mm", layer=None,
        col0=0, n=None):
    m, kdim = a.shape
    n = w.shape[-1] - col0 if n is None else n
    tm = min(tm, m)
    tk = kdim if tk is None else tk
    nk = kdim // tk
    cb0 = col0 // tn
    assert col0 % tn == 0 and n % tn == 0 and kdim % tk == 0
    if layer is None:
        w_spec = pl.BlockSpec((tk, tn), lambda i, j, k: (k, j + cb0))
    else:
        w_spec = pl.BlockSpec((None, tk, tn), lambda i, j, k: (layer, k, j + cb0))
    in_specs = [pl.BlockSpec((tm, tk), lambda i, j, k: (i, k)), w_spec]
    args = [a, w]
    if mode == "residual":
        in_specs.append(pl.BlockSpec((tm, tn), lambda i, j, k: (i, j)))
        args.append(extra)
    elif mode == "rmsnorm":
        assert tn == n
        in_specs.append(pl.BlockSpec((1, tn), lambda i, j, k: (0, j)))
        args.append(extra.reshape(1, n))
    scratch = [pltpu.VMEM((tm, tn), jnp.float32)] if (nk > 1 or mode == "transpose") else []
    if mode == "transpose":
        out_shape = jax.ShapeDtypeStruct((n, m), out_dtype)
        out_spec = pl.BlockSpec((tn, tm), lambda i, j, k: (j, i))
    else:
        out_shape = jax.ShapeDtypeStruct((m, n), out_dtype)
        out_spec = pl.BlockSpec((tm, tn), lambda i, j, k: (i, j))
    return pl.pallas_call(
        functools.partial(_mm_kernel, nk=nk, mode=mode),
        out_shape=out_shape,
        grid=(m // tm, n // tn, nk),
        in_specs=in_specs,
        out_specs=out_spec,
        scratch_shapes=scratch,
        compiler_params=_cparams(("parallel", "parallel", "arbitrary")),
        name=name,
    )(*args)


def _pair_expand(cols, j, lane_lo):
    return jnp.where(lane_lo, cols[:, 2 * j:2 * j + 1], cols[:, 2 * j + 1:2 * j + 2])


def _ssd_kernel(z_ref, x_ref, b_ref, c_ref, cwx_ref, cwb_ref, cwc_ref, cbx_ref, cbb_ref, cbc_ref,
                dtc_ref, dtr_ref, biasc_ref, biasr_ref, alogc_ref, alogr_ref, dskip_ref, ng_ref,
                o_ref, xbuf, bbuf, cbuf, state):
    gw = x_ref.shape[1] // SSD_GROUPS_PER_STEP
    n = b_ref.shape[1] // SSD_GROUPS_PER_STEP
    for g in range(SSD_GROUPS_PER_STEP):
        wide = (slice(None), slice(g * gw, (g + 1) * gw))
        narrow = (slice(None), slice(g * n, (g + 1) * n))
        _ssd_group(z_ref.at[wide], x_ref.at[wide], b_ref.at[narrow], c_ref.at[narrow],
                   cwx_ref.at[wide], cwb_ref.at[narrow], cwc_ref.at[narrow],
                   cbx_ref.at[wide], cbb_ref.at[narrow], cbc_ref.at[narrow],
                   dtc_ref.at[g], dtr_ref.at[g], biasc_ref.at[g], biasr_ref.at[g], alogc_ref.at[g],
                   alogr_ref.at[g], dskip_ref.at[g], ng_ref.at[wide], o_ref.at[wide],
                   xbuf.at[g], bbuf.at[g], cbuf.at[g], state.at[g])


def _ssd_group(z_ref, x_ref, b_ref, c_ref, cwx_ref, cwb_ref, cwc_ref, cbx_ref, cbb_ref, cbc_ref,
               dtc_ref, dtr_ref, biasc_ref, biasr_ref, alogc_ref, alogr_ref, dskip_ref, ng_ref,
               o_ref, xbuf, bbuf, cbuf, state):
    L = SSD_L
    c = pl.program_id(1)

    @pl.when(c == 0)
    def _():
        xbuf[0:CONV_PAD, :] = jnp.zeros((CONV_PAD, xbuf.shape[1]), jnp.float32)
        bbuf[0:CONV_PAD, :] = jnp.zeros((CONV_PAD, bbuf.shape[1]), jnp.float32)
        cbuf[0:CONV_PAD, :] = jnp.zeros((CONV_PAD, cbuf.shape[1]), jnp.float32)
        state[...] = jnp.zeros_like(state)

    def conv_silu(src_ref, buf, w_ref, bias_ref):
        buf[CONV_PAD:CONV_PAD + L, :] = src_ref[...].astype(jnp.float32)
        acc = bias_ref[...] + w_ref[CONV_WIDTH - 1:CONV_WIDTH, :] * buf[CONV_PAD:CONV_PAD + L, :]
        for k in range(CONV_WIDTH - 1):
            off = CONV_PAD - (CONV_WIDTH - 1) + k
            acc = acc + w_ref[k:k + 1, :] * buf[off:off + L, :]
        buf[0:CONV_PAD, :] = buf[L:L + CONV_PAD, :]
        return _silu(acc)

    xs = conv_silu(x_ref, xbuf, cwx_ref, cbx_ref)
    bs = conv_silu(b_ref, bbuf, cwb_ref, cbb_ref)
    cs = conv_silu(c_ref, cbuf, cwc_ref, cbc_ref)
    bs16 = _bf16(bs)
    cs16 = _bf16(cs)

    dtc = _softplus(dtc_ref[...] + biasc_ref[...])
    dtr = _softplus(dtr_ref[...] + biasr_ref[...])
    a_c = dtc * (-jnp.exp(alogc_ref[...]))
    a_r = dtr * (-jnp.exp(alogr_ref[...]))

    rid = lax.broadcasted_iota(jnp.int32, (L, L), 0)
    cid = lax.broadcasted_iota(jnp.int32, (L, L), 1)
    causal = rid >= cid
    tril = _bf16(jnp.where(causal, 1.0, 0.0))
    triu = _bf16(jnp.where(rid <= cid, 1.0, 0.0))
    acum_c = sum(_dot(tril, t) for t in _split3(a_c))
    acum_r = sum(_dot(t, triu) for t in _split3(a_r))

    total_c = acum_c[L - 1:L, :]
    e_c = jnp.exp(acum_c)
    w_c = dtc * jnp.exp(total_c - acum_c)
    etot = jnp.exp(total_c)

    cb = lax.dot_general(cs16, bs16, (((1,), (1,)), ((), ())),
                         preferred_element_type=jnp.float32)
    st = state[...]
    yoff = _dot(cs16, _bf16(st))

    lane = lax.broadcasted_iota(jnp.int32, (L, LANES), 1)
    lane_lo = lane < SSM_HEAD_DIM
    lane1_lo = lax.broadcasted_iota(jnp.int32, (1, LANES), 1) < SSM_HEAD_DIM

    acum_c2 = acum_c * LOG2E
    row_exp2 = acum_r * LOG2E - jnp.log2(dtr)

    z = z_ref[...].astype(jnp.float32)
    n_pairs = xs.shape[1] // LANES
    ys = []
    xws = []
    for j in range(n_pairs):
        xp = xs[:, j * LANES:(j + 1) * LANES]
        ms = []
        for hh in (2 * j, 2 * j + 1):
            seg = acum_c2[:, hh:hh + 1] - row_exp2[hh:hh + 1, :]
            ms.append(_bf16(cb * jnp.where(causal, jnp.exp2(seg), 0.0)))
        mpair = jnp.concatenate(ms, axis=1)
        rhs = jnp.concatenate([jnp.where(lane_lo, xp, 0.0), jnp.where(lane_lo, 0.0, xp)], axis=0)
        ydiag = _dot(mpair, _bf16(rhs))
        y = ydiag + yoff[:, j * LANES:(j + 1) * LANES] * _pair_expand(e_c, j, lane_lo)
        y = y + xp * dskip_ref[:, j * LANES:(j + 1) * LANES]
        ys.append(y)
        xws.append(_bf16(xp * _pair_expand(w_c, j, lane_lo)))
    y = jnp.concatenate(ys, axis=1)
    xw = jnp.concatenate(xws, axis=1)

    snew = lax.dot_general(bs16, xw, (((0,), (0,)), ((), ())),
                           preferred_element_type=jnp.float32)
    dec_row = jnp.concatenate([_pair_expand(etot, j, lane1_lo) for j in range(n_pairs)], axis=1)
    state[...] = st * dec_row + snew

    y = y * _silu(z)
    y = y * lax.rsqrt(jnp.mean(y * y, axis=-1, keepdims=True) + EPS)
    o_ref[...] = _bf16(y * ng_ref[...])


def _ssd(proj, dt_raw, conv_w, conv_b, dt_bias, a_log, d_skip, norm_g):
    s = proj.shape[0]
    g = SSM_GROUPS
    heads = dt_raw.shape[1]
    hg = heads // g
    gw = hg * SSM_HEAD_DIM
    d_inner = g * gw
    n = SSM_STATE
    L = SSD_L
    xb = d_inner // gw
    bb = 2 * d_inner // n
    cb_ = bb + g
    cwb = d_inner // n
    cwc = cwb + g

    dtc = dt_raw.reshape(s, g, hg).transpose(1, 0, 2)
    dtr = dt_raw.reshape(s, g, hg).transpose(1, 2, 0)
    cw = conv_w.reshape(CONV_WIDTH, -1)
    cbias = conv_b.reshape(1, -1)
    dsk = jnp.repeat(d_skip.reshape(g, 1, hg), SSM_HEAD_DIM, axis=2)

    def gspec(shape, fn):
        return pl.BlockSpec(shape, fn)

    gs = SSD_GROUPS_PER_STEP
    assert g % gs == 0 and xb % gs == 0 and bb % gs == 0 and cb_ % gs == 0 and cwb % gs == 0 and cwc % gs == 0
    in_specs = [
        gspec((L, gs * gw), lambda gi, c: (c, gi)),
        gspec((L, gs * gw), lambda gi, c: (c, xb // gs + gi)),
        gspec((L, gs * n), lambda gi, c: (c, bb // gs + gi)),
        gspec((L, gs * n), lambda gi, c: (c, cb_ // gs + gi)),
        gspec((CONV_WIDTH, gs * gw), lambda gi, c: (0, gi)),
        gspec((CONV_WIDTH, gs * n), lambda gi, c: (0, cwb // gs + gi)),
        gspec((CONV_WIDTH, gs * n), lambda gi, c: (0, cwc // gs + gi)),
        gspec((1, gs * gw), lambda gi, c: (0, gi)),
        gspec((1, gs * n), lambda gi, c: (0, cwb // gs + gi)),
        gspec((1, gs * n), lambda gi, c: (0, cwc // gs + gi)),
        gspec((gs, L, hg), lambda gi, c: (gi, c, 0)),
        gspec((gs, hg, L), lambda gi, c: (gi, 0, c)),
        gspec((gs, 1, hg), lambda gi, c: (gi, 0, 0)),
        gspec((gs, hg, 1), lambda gi, c: (gi, 0, 0)),
        gspec((gs, 1, hg), lambda gi, c: (gi, 0, 0)),
        gspec((gs, hg, 1), lambda gi, c: (gi, 0, 0)),
        gspec((gs, 1, gw), lambda gi, c: (gi, 0, 0)),
        gspec((1, gs * gw), lambda gi, c: (0, gi)),
    ]
    return pl.pallas_call(
        _ssd_kernel,
        out_shape=jax.ShapeDtypeStruct((s, d_inner), jnp.bfloat16),
        grid=(g // gs, s // L),
        in_specs=in_specs,
        out_specs=pl.BlockSpec((L, gs * gw), lambda gi, c: (c, gi)),
        scratch_shapes=[pltpu.VMEM((gs, CONV_PAD + L, gw), jnp.float32),
                        pltpu.VMEM((gs, CONV_PAD + L, n), jnp.float32),
                        pltpu.VMEM((gs, CONV_PAD + L, n), jnp.float32),
                        pltpu.VMEM((gs, n, gw), jnp.float32)],
        compiler_params=_cparams(("parallel", "arbitrary")),
        name="ssd",
    )(proj, proj, proj, proj, cw, cw, cw, cbias, cbias, cbias, dtc, dtr,
      dt_bias.reshape(g, 1, hg), dt_bias.reshape(g, hg, 1),
      a_log.reshape(g, 1, hg), a_log.reshape(g, hg, 1), dsk, norm_g.reshape(1, d_inner))


def _mamba_layer(x, h, w_in_all, conv_w, conv_b, dt_bias, a_log, d_skip, norm_g, w_out_all, layer):
    d_inner = w_out_all.shape[1]
    n_main = 2 * d_inner + 2 * SSM_GROUPS * SSM_STATE
    n_dt = w_in_all.shape[2] - n_main
    proj = _mm(h, w_in_all, layer=layer, n=n_main, out_dtype=jnp.bfloat16, tm=1024, tn=512,
               name="ssm_in_proj")
    dt_raw = _mm(h, w_in_all, layer=layer, col0=n_main, n=n_dt, out_dtype=jnp.float32, tm=1024, tn=n_dt,
                 name="ssm_dt_proj")
    y = _ssd(proj, dt_raw, conv_w, conv_b, dt_bias, a_log, d_skip, norm_g)
    return _mm(y, w_out_all, layer=layer, out_dtype=jnp.float32, tm=1024, tn=1024, tk=2048,
               mode="residual", extra=x, name="ssm_out_proj")


def _router_kernel(x_ref, g_ref, whi_ref, wlo_ref, oi_ref, of_ref, cnt_ref, hp_ref, carry):
    i = pl.program_id(0)
    tm = x_ref.shape[0]

    @pl.when(i == 0)
    def _():
        carry[...] = jnp.zeros_like(carry)

    x = x_ref[...]
    h = x * lax.rsqrt(jnp.mean(x * x, axis=-1, keepdims=True) + EPS) * g_ref[...]
    hp_ref[...] = _pack_bf16_pairs(h)
    hhi = _bf16(h)
    hlo = _bf16(h - hhi.astype(jnp.float32))
    whi = whi_ref[...]
    logits = _dot(hhi, whi) + (_dot(hlo, whi) + _dot(hhi, wlo_ref[...]))

    lane = lax.broadcasted_iota(jnp.int32, (tm, LANES), 1)
    ninf = -jnp.inf
    gmask = lane < N_GROUPS
    gl = jnp.where(gmask, logits, ninf)
    gmax = jnp.max(gl, axis=-1, keepdims=True)
    gsel = jnp.min(jnp.where(gl == gmax, lane, LANES), axis=-1, keepdims=True)
    psel = 1.0 / jnp.sum(jnp.where(gmask, jnp.exp(logits - gmax), 0.0), axis=-1, keepdims=True)
    lo = N_GROUPS + EXPERTS_PER_GROUP * gsel
    emask = (lane >= lo) & (lane < lo + EXPERTS_PER_GROUP)
    el = jnp.where(emask, logits, ninf)
    v1 = jnp.max(el, axis=-1, keepdims=True)
    i1 = jnp.min(jnp.where(el == v1, lane, LANES), axis=-1, keepdims=True)
    el2 = jnp.where(lane == i1, ninf, el)
    v2 = jnp.max(el2, axis=-1, keepdims=True)
    i2 = jnp.min(jnp.where(el2 == v2, lane, LANES), axis=-1, keepdims=True)
    e21 = jnp.exp(v2 - v1)
    w1 = psel / (1.0 + e21)
    w2 = psel * e21 / (1.0 + e21)

    hit1 = lane == i1
    hit2 = lane == i2
    onehot = jnp.where(hit1 | hit2, 1.0, 0.0)
    rid = lax.broadcasted_iota(jnp.int32, (tm, tm), 0)
    cid = lax.broadcasted_iota(jnp.int32, (tm, tm), 1)
    strict = _bf16(jnp.where(rid > cid, 1.0, 0.0))
    before = _dot(strict, _bf16(onehot)) + carry[...]
    r1 = jnp.sum(jnp.where(hit1, before, 0.0), axis=-1, keepdims=True)
    r2 = jnp.sum(jnp.where(hit2, before, 0.0), axis=-1, keepdims=True)
    carry[...] += jnp.sum(onehot, axis=0, keepdims=True)

    e1 = i1 - N_GROUPS
    e2 = i2 - N_GROUPS
    oi = jnp.where(lane == 0, e1, jnp.where(lane == 1, e2, jnp.where(
        lane == 2, r1.astype(jnp.int32), jnp.where(lane == 3, r2.astype(jnp.int32), 0))))
    oi_ref[...] = oi
    of_ref[...] = jnp.where(lane == 0, w1, jnp.where(lane == 1, w2, 0.0))
    cnt_ref[...] = carry[...]


def _router(x, g_ffn, w_group, w_expert, tm=256):
    s, d = x.shape
    wr = jnp.concatenate([w_group, w_expert], axis=1)
    wr = jnp.pad(wr, ((0, 0), (0, LANES - wr.shape[1])))
    whi = _bf16(wr)
    wlo = _bf16(wr - whi.astype(jnp.float32))
    return pl.pallas_call(
        _router_kernel,
        out_shape=(jax.ShapeDtypeStruct((s, LANES), jnp.int32),
                   jax.ShapeDtypeStruct((s, LANES), jnp.float32),
                   jax.ShapeDtypeStruct((1, LANES), jnp.float32),
                   jax.ShapeDtypeStruct((s, d // 2), jnp.uint32)),
        grid=(s // tm,),
        in_specs=[pl.BlockSpec((tm, d), lambda i: (i, 0)),
                  pl.BlockSpec((1, d), lambda i: (0, 0)),
                  pl.BlockSpec((d, LANES), lambda i: (0, 0)),
                  pl.BlockSpec((d, LANES), lambda i: (0, 0))],
        out_specs=(pl.BlockSpec((tm, LANES), lambda i: (i, 0)),
                   pl.BlockSpec((tm, LANES), lambda i: (i, 0)),
                   pl.BlockSpec((1, LANES), lambda i: (0, 0)),
                   pl.BlockSpec((tm, d // 2), lambda i: (i, 0))),
        scratch_shapes=[pltpu.VMEM((1, LANES), jnp.float32)],
        compiler_params=_cparams(("arbitrary",)),
        name="moe_router",
    )(x, g_ffn.reshape(1, d), whi, wlo)


def _invert_kernel(pos_ref, src_ref):
    def clear(r, carry):
        src_ref[r] = 0
        return carry

    def place(a, carry):
        src_ref[pos_ref[a]] = lax.shift_right_logical(a, 1)
        return carry

    lax.fori_loop(0, src_ref.shape[0], clear, 0, unroll=8)
    lax.fori_loop(0, pos_ref.shape[0], place, 0, unroll=8)


def _invert(pos_flat, n_rows):
    assert TOP_K == 2
    return pl.pallas_call(
        _invert_kernel,
        out_shape=jax.ShapeDtypeStruct((n_rows,), jnp.int32),
        in_specs=[pl.BlockSpec(memory_space=pltpu.SMEM)],
        out_specs=pl.BlockSpec(memory_space=pltpu.SMEM),
        name="moe_invert",
    )(pos_flat)


def _expert_kernel(te_ref, tv_ref, first_ref, nexte_ref, wslot_ref, src_ref, x_hbm, wg_hbm, wu_hbm, wd_hbm,
                   ys_ref, xbuf, wbuf_g, wbuf_u, wbuf_d, wg16, wu16, wd16, sem, wsem, *, layer):
    i = pl.program_id(0)
    nt = pl.num_programs(0)
    tm = xbuf.shape[1]

    def gather(tile, slot):
        def issue(rr, carry):
            for u in range(2):
                r = 2 * rr + u
                tok = src_ref[tile * tm + r]
                pltpu.make_async_copy(x_hbm.at[pl.ds(tok, 1)], xbuf.at[slot, pl.ds(r, 1)],
                                      sem.at[slot]).start(priority=u)
            return carry
        lax.fori_loop(0, tm // 2, issue, 0, unroll=4)

    def weight_copies(e, slot):
        return (pltpu.make_async_copy(wg_hbm.at[layer, e], wbuf_g.at[slot], wsem.at[slot, 0]),
                pltpu.make_async_copy(wu_hbm.at[layer, e], wbuf_u.at[slot], wsem.at[slot, 1]),
                pltpu.make_async_copy(wd_hbm.at[layer, e], wbuf_d.at[slot], wsem.at[slot, 2]))

    @pl.when(i == 0)
    def _():
        gather(0, 0)
        for c in weight_copies(te_ref[0], 0):
            c.start()

    nxt = jnp.minimum(i + 1, nt - 1)

    @pl.when((i + 1 < nt) & (tv_ref[nxt] > 0))
    def _():
        gather(i + 1, (i + 1) % 2)

    @pl.when(first_ref[i] > 0)
    def _():
        ws = wslot_ref[i]
        for c in weight_copies(te_ref[i], ws):
            c.wait()
        ne = nexte_ref[i]

        @pl.when(ne >= 0)
        def _():
            for c in weight_copies(ne, 1 - ws):
                c.start()

        wg16[...] = _bf16(wbuf_g[ws])
        wu16[...] = _bf16(wbuf_u[ws])
        wd16[...] = _bf16(wbuf_d[ws])

    @pl.when(tv_ref[i] > 0)
    def _():
        slot = i % 2
        pltpu.make_async_copy(x_hbm.at[pl.ds(0, tm)], xbuf.at[slot], sem.at[slot]).wait()
        half = xbuf.shape[2]
        h_hi, h_lo = (_bf16(t) for t in _unpack_bf16_pairs(xbuf[slot]))
        gate = _dot(h_hi, wg16[0:half, :]) + _dot(h_lo, wg16[half:, :])
        up = _dot(h_hi, wu16[0:half, :]) + _dot(h_lo, wu16[half:, :])
        hid = _bf16(_silu(gate) * up)
        ys_ref[...] = _pack_bf16_pairs(_dot(hid, wd16[...]))

    @pl.when(tv_ref[i] == 0)
    def _():
        ys_ref[...] = jnp.zeros_like(ys_ref)


def _experts(hp, src, w_gate, w_up, w_down, layer, tile_e, tile_v, counts):
    dh = hp.shape[1]
    d = 2 * dh
    ff = w_gate.shape[-1]
    n_rows = src.shape[0]
    nt = n_rows // MOE_TM
    idx = jnp.arange(nt, dtype=jnp.int32)
    prev_e = jnp.concatenate([jnp.full((1,), -1, jnp.int32), tile_e[:-1]])
    first = ((tile_v > 0) & ((idx == 0) | (tile_e != prev_e))).astype(jnp.int32)
    wslot = (jnp.cumsum(first) - 1) & 1
    eids = jnp.arange(N_EXPERTS, dtype=jnp.int32)
    present = jnp.where(counts > 0, eids, N_EXPERTS)
    at_or_after = lax.cummin(present, reverse=True)
    after = jnp.concatenate([at_or_after[1:], jnp.full((1,), N_EXPERTS, jnp.int32)])
    after = jnp.where(after >= N_EXPERTS, -1, after)
    nexte = jnp.where(first > 0, after[tile_e], -1).astype(jnp.int32)
    any_spec = pl.BlockSpec(memory_space=pl.ANY)
    return pl.pallas_call(
        functools.partial(_expert_kernel, layer=layer),
        out_shape=jax.ShapeDtypeStruct((n_rows, dh), jnp.uint32),
        grid_spec=pltpu.PrefetchScalarGridSpec(
            num_scalar_prefetch=6,
            grid=(nt,),
            in_specs=[any_spec, any_spec, any_spec, any_spec],
            out_specs=pl.BlockSpec((MOE_TM, dh), lambda i, *_: (i, 0)),
            scratch_shapes=[pltpu.VMEM((2, MOE_TM, dh), jnp.uint32),
                            pltpu.VMEM((2, d, ff), jnp.float32),
                            pltpu.VMEM((2, d, ff), jnp.float32),
                            pltpu.VMEM((2, ff, d), jnp.float32),
                            pltpu.VMEM((d, ff), jnp.bfloat16),
                            pltpu.VMEM((d, ff), jnp.bfloat16),
                            pltpu.VMEM((ff, d), jnp.bfloat16),
                            pltpu.SemaphoreType.DMA((2,)),
                            pltpu.SemaphoreType.DMA((2, 3))]),
        compiler_params=_cparams(("arbitrary",)),
        name="moe_experts",
    )(tile_e, tile_v, first, nexte, wslot.astype(jnp.int32), src, hp, w_gate, w_up, w_down)


def _combine_kernel(pos_ref, x_ref, wt_ref, ys_hbm, *rest, tc, n_norm):
    gain_refs = rest[:n_norm]
    o_ref = rest[n_norm]
    h_refs = rest[n_norm + 1:2 * n_norm + 1]
    buf, sem = rest[2 * n_norm + 1:]
    i = pl.program_id(0)
    n = pl.num_programs(0)

    def gather(step, slot):
        def issue(t, carry):
            for k in range(TOP_K):
                p = pos_ref[TOP_K * (step * tc + t) + k]
                pltpu.make_async_copy(ys_hbm.at[pl.ds(p, 1)], buf.at[slot, k, pl.ds(t, 1)],
                                      sem.at[slot]).start(priority=k % 2)
            return carry
        lax.fori_loop(0, tc, issue, 0, unroll=8)

    @pl.when(i == 0)
    def _():
        gather(0, 0)

    @pl.when(i + 1 < n)
    def _():
        gather(i + 1, (i + 1) % 2)

    slot = i % 2
    for k in range(TOP_K):
        pltpu.make_async_copy(ys_hbm.at[pl.ds(0, tc)], buf.at[slot, k], sem.at[slot]).wait()
    half = buf.shape[3]
    rows_per_chunk = 32

    def chunk(c, carry):
        rows = pl.ds(pl.multiple_of(c * rows_per_chunk, rows_per_chunk), rows_per_chunk)
        wt = wt_ref[rows, :]
        y0_hi, y0_lo = _unpack_bf16_pairs(buf[slot, 0, rows, :])
        y1_hi, y1_lo = _unpack_bf16_pairs(buf[slot, 1, rows, :])
        out_hi = x_ref[rows, 0:half] + wt[:, 0:1] * y0_hi + wt[:, 1:2] * y1_hi
        out_lo = x_ref[rows, half:] + wt[:, 0:1] * y0_lo + wt[:, 1:2] * y1_lo
        o_ref[rows, 0:half] = out_hi
        o_ref[rows, half:] = out_lo
        if n_norm:
            ss = (jnp.sum(out_hi * out_hi, axis=-1, keepdims=True)
                  + jnp.sum(out_lo * out_lo, axis=-1, keepdims=True))
            scale = lax.rsqrt(ss * (1.0 / (2 * half)) + EPS)
            for g_ref, h_ref in zip(gain_refs, h_refs):
                h_ref[rows, 0:half] = _bf16(out_hi * scale * g_ref[:, 0:half])
                h_ref[rows, half:] = _bf16(out_lo * scale * g_ref[:, half:])
        return carry

    lax.fori_loop(0, tc // rows_per_chunk, chunk, 0)


def _combine(x, wts, ys, pos_flat, gains, tc=256):
    s, d = x.shape
    n_norm = len(gains)
    row_spec = pl.BlockSpec((tc, d), lambda i, pos: (i, 0))
    gain_spec = pl.BlockSpec((1, d), lambda i, pos: (0, 0))
    outs = pl.pallas_call(
        functools.partial(_combine_kernel, tc=tc, n_norm=n_norm),
        out_shape=(jax.ShapeDtypeStruct((s, d), jnp.float32),)
        + (jax.ShapeDtypeStruct((s, d), jnp.bfloat16),) * n_norm,
        grid_spec=pltpu.PrefetchScalarGridSpec(
            num_scalar_prefetch=1,
            grid=(s // tc,),
            in_specs=[row_spec, pl.BlockSpec((tc, LANES), lambda i, pos: (i, 0)),
                      pl.BlockSpec(memory_space=pl.ANY)] + [gain_spec] * n_norm,
            out_specs=(row_spec,) * (1 + n_norm),
            scratch_shapes=[pltpu.VMEM((2, TOP_K, tc, d // 2), jnp.uint32),
                            pltpu.SemaphoreType.DMA((2,))]),
        compiler_params=_cparams(("arbitrary",)),
        name="moe_combine",
    )(pos_flat, x, wts, ys, *[g.reshape(1, d) for g in gains])
    return outs[0], list(outs[1:])


def _moe_layer(x, g_ffn, w_group, w_expert, w_gate, w_up, w_down, layer, next_gains):
    s = x.shape[0]
    oi, wts, cnt, hp = _router(x, g_ffn, w_group, w_expert)
    e = oi[:, 0:TOP_K]
    rank = oi[:, TOP_K:2 * TOP_K]
    counts = cnt[0, N_GROUPS:N_GROUPS + N_EXPERTS].astype(jnp.int32)
    tiles = (counts + MOE_TM - 1) // MOE_TM
    tile_end = jnp.cumsum(tiles)
    row_off = (tile_end - tiles) * MOE_TM
    pos = (row_off[e] + rank).reshape(-1).astype(jnp.int32)
    nt = s * TOP_K // MOE_TM + N_EXPERTS
    tidx = jnp.minimum(jnp.arange(nt, dtype=jnp.int32), tile_end[-1] - 1)
    tile_e = jnp.minimum(jnp.searchsorted(tile_end, tidx, side="right"), N_EXPERTS - 1)
    tile_v = (jnp.arange(nt, dtype=jnp.int32) < tile_end[-1]).astype(jnp.int32)
    src = _invert(pos, nt * MOE_TM)
    ys = _experts(hp, src, w_gate, w_up, w_down, layer, tile_e.astype(jnp.int32), tile_v, counts)
    return _combine(x, wts, ys, pos, next_gains)


def _rope_table_kernel(pos_ref, inv_ref, cos_ref, sin_ref):
    ang = pos_ref[...].astype(jnp.float32) * inv_ref[...]
    cos_ref[...] = jnp.cos(ang)
    sin_ref[...] = jnp.sin(ang)


def _rope_tables(pos_col, tm=512):
    s = pos_col.shape[0]
    half = ROPE_DIM // 2
    inv = jnp.power(ROPE_THETA, -jnp.arange(0, ROPE_DIM, 2, dtype=jnp.float32) / ROPE_DIM)
    inv = jnp.concatenate([inv, inv, jnp.zeros((LANES - 2 * half,), jnp.float32)]).reshape(1, LANES)
    return pl.pallas_call(
        _rope_table_kernel,
        out_shape=(jax.ShapeDtypeStruct((s, LANES), jnp.float32),) * 2,
        grid=(s // tm,),
        in_specs=[pl.BlockSpec((tm, 1), lambda i: (i, 0)),
                  pl.BlockSpec((1, LANES), lambda i: (0, 0))],
        out_specs=(pl.BlockSpec((tm, LANES), lambda i: (i, 0)),) * 2,
        compiler_params=_cparams(("parallel",)),
        name="rope_tables",
    )(pos_col, inv)


def _rot_half_matrix():
    half = ROPE_DIM // 2
    r = jnp.arange(LANES)[:, None]
    c = jnp.arange(LANES)[None, :]
    plus = (c >= half) & (c < ROPE_DIM) & (r == c - half)
    minus = (c < half) & (r == c + half)
    return _bf16(plus.astype(jnp.float32) - minus.astype(jnp.float32))


def _head_norm_rope(o_ref, h, nope, rope128, gn, gr128, cos, sin, rot_m, out_scale):
    ones = jnp.ones((LANES, LANES), jnp.bfloat16)
    ss = _dot(_bf16(nope * nope), ones) + _dot(_bf16(rope128 * rope128), ones)
    scale = lax.rsqrt(ss * (1.0 / QK_DIM) + EPS) * out_scale
    rg = rope128 * gr128
    roped = rg * cos + _dot(_bf16(rg), rot_m) * sin
    o_ref[h, :, 0:NOPE_DIM] = _bf16(nope * gn * scale)
    o_ref[h, :, NOPE_DIM:QK_DIM] = _bf16((roped * scale)[:, :ROPE_DIM])


def _kbuild_kernel(ckv_ref, w_ref, kr_ref, gn_ref, gr_ref, cos_ref, sin_ref, rot_ref, o_ref):
    kns = [_dot(ckv_ref[...], w_ref[h]) for h in range(w_ref.shape[0])]
    for h, kn in enumerate(kns):
        _head_norm_rope(o_ref, h, kn, kr_ref[...], gn_ref[...], gr_ref[...], cos_ref[...], sin_ref[...],
                        rot_ref[...], 1.0)


def _qbuild_kernel(qa_ref, w_ref, gn_ref, gr_ref, cos_ref, sin_ref, rot_ref, o_ref):
    qs = [_dot(qa_ref[...], w_ref[h]) for h in range(w_ref.shape[0])]
    for h, q in enumerate(qs):
        _head_norm_rope(o_ref, h, q[:, :NOPE_DIM], q[:, NOPE_DIM:], gn_ref[...], gr_ref[...],
                        cos_ref[...], sin_ref[...], rot_ref[...], Q_SCALE)


def _norm_gain_split(g):
    gn = g[:NOPE_DIM].reshape(1, NOPE_DIM)
    gr = jnp.pad(g[NOPE_DIM:], (0, LANES - ROPE_DIM)).reshape(1, LANES)
    return gn, gr


def _kbuild(ckv, w_uk, kr_raw, k_norm_g, cos, sin, tm=512, hb=HEADS_PER_STEP):
    s, r = ckv.shape
    tm = min(tm, s)
    heads = w_uk.shape[1] // NOPE_DIM
    w = _bf16(w_uk).reshape(r, heads, NOPE_DIM).transpose(1, 0, 2)
    gn, gr = _norm_gain_split(k_norm_g)
    return pl.pallas_call(
        _kbuild_kernel,
        out_shape=jax.ShapeDtypeStruct((heads, s, QK_DIM), jnp.bfloat16),
        grid=(s // tm, heads // hb),
        in_specs=[pl.BlockSpec((tm, r), lambda i, h: (i, 0)),
                  pl.BlockSpec((hb, r, NOPE_DIM), lambda i, h: (h, 0, 0)),
                  pl.BlockSpec((tm, LANES), lambda i, h: (i, 0)),
                  pl.BlockSpec((1, NOPE_DIM), lambda i, h: (0, 0)),
                  pl.BlockSpec((1, LANES), lambda i, h: (0, 0)),
                  pl.BlockSpec((tm, LANES), lambda i, h: (i, 0)),
                  pl.BlockSpec((tm, LANES), lambda i, h: (i, 0)),
                  pl.BlockSpec((LANES, LANES), lambda i, h: (0, 0))],
        out_specs=pl.BlockSpec((hb, tm, QK_DIM), lambda i, h: (h, i, 0)),
        compiler_params=_cparams(("parallel", "arbitrary")),
        name="mla_k_build",
    )(ckv, w, kr_raw, gn, gr, cos, sin, _rot_half_matrix())


def _qbuild(qa, w_uq, q_norm_g, cos, sin, tm=512, hb=HEADS_PER_STEP):
    s, r = qa.shape
    tm = min(tm, s)
    heads = w_uq.shape[1] // QK_DIM
    w = _bf16(w_uq).reshape(r, heads, QK_DIM).transpose(1, 0, 2)
    w = jnp.pad(w, ((0, 0), (0, 0), (0, 2 * LANES - QK_DIM)))
    gn, gr = _norm_gain_split(q_norm_g)
    return pl.pallas_call(
        _qbuild_kernel,
        out_shape=jax.ShapeDtypeStruct((heads, s, QK_DIM), jnp.bfloat16),
        grid=(s // tm, heads // hb),
        in_specs=[pl.BlockSpec((tm, r), lambda i, h: (i, 0)),
                  pl.BlockSpec((hb, r, 2 * LANES), lambda i, h: (h, 0, 0)),
                  pl.BlockSpec((1, NOPE_DIM), lambda i, h: (0, 0)),
                  pl.BlockSpec((1, LANES), lambda i, h: (0, 0)),
                  pl.BlockSpec((tm, LANES), lambda i, h: (i, 0)),
                  pl.BlockSpec((tm, LANES), lambda i, h: (i, 0)),
                  pl.BlockSpec((LANES, LANES), lambda i, h: (0, 0))],
        out_specs=pl.BlockSpec((hb, tm, QK_DIM), lambda i, h: (h, i, 0)),
        compiler_params=_cparams(("parallel", "arbitrary")),
        name="mla_q_build",
    )(qa, w, gn, gr, cos, sin, _rot_half_matrix())


def _reduce_rows(x, pair_op, final_op):
    rows = x.shape[0]
    while rows > 8:
        rows //= 2
        x = pair_op(x[:rows], x[rows:])
    return final_op(x, axis=0, keepdims=True)


def _attn_kernel(nfull_ref, nkv_ref, q_ref, k_ref, vt_ref, kpos_ref, qpos_ref, o_ref, *, tk, n_sub,
                 group_shift):
    i = pl.program_id(1)
    group = 1 << group_shift
    sub = q_ref.shape[0] // n_sub
    qs = [q_ref[h * sub:(h + 1) * sub, :] for h in range(n_sub)]
    qcs = [qpos_ref[:, h * sub:(h + 1) * sub] >> CHUNK_SHIFT for h in range(n_sub)]

    def tile_stats(pieces, masked):
        chains = [(t, h) for t in range(len(pieces)) for h in range(n_sub)]
        scores = {}
        for t, h in chains:
            off, size = pieces[t]
            k = k_ref[pl.ds(off, size), :]
            s = lax.dot_general(k, qs[h], (((1,), (1,)), ((), ())),
                                preferred_element_type=jnp.float32)
            if masked:
                kc = kpos_ref[pl.ds(off, size), :] >> CHUNK_SHIFT
                s = jnp.where(kc <= qcs[h], s, NEG_BIG)
            scores[t, h] = s
        soft = {}
        for t, h in chains:
            m_t = _reduce_rows(scores[t, h], jnp.maximum, jnp.max)
            p = jnp.exp2(scores[t, h] - m_t)
            soft[t, h] = (m_t, _reduce_rows(p, jnp.add, jnp.sum), _bf16(p))
        out = [[None] * n_sub for _ in pieces]
        for t, h in chains:
            off, size = pieces[t]
            vt = vt_ref[:, pl.ds(off, size)]
            out[t][h] = (soft[t, h][0], soft[t, h][1], _dot(vt, soft[t, h][2]))
        return out

    def whole(j):
        return (pl.multiple_of(j * tk, tk), tk)

    def halves(j):
        return [(pl.multiple_of(j * tk + u * (tk // 2), tk // 2), tk // 2) for u in range(2)]

    def merge(carry, tiles):
        out = []
        for h in range(n_sub):
            m, l, acc = carry[h]
            m_new = m
            for t in tiles:
                m_new = jnp.maximum(m_new, t[h][0])
            a = jnp.exp2(m - m_new)
            l, acc = l * a, acc * a
            for t in tiles:
                b = jnp.exp2(t[h][0] - m_new)
                l, acc = l + t[h][1] * b, acc + t[h][2] * b
            out.append((m_new, l, acc))
        return tuple(out)

    def group_step(jj, carry):
        return merge(carry, tile_stats([whole(group * jj + u) for u in range(group)], False))

    def single_step(j, carry):
        return merge(carry, tile_stats(halves(j), False))

    def masked_step(j, carry):
        return merge(carry, tile_stats(halves(j), True))

    init = tuple((jnp.full((1, sub), NEG_BIG, jnp.float32), jnp.zeros((1, sub), jnp.float32),
                  jnp.zeros((V_DIM, sub), jnp.float32)) for _ in range(n_sub))
    def pair_step(jj, carry):
        return merge(carry, tile_stats([whole(2 * jj), whole(2 * jj + 1)], False))

    nfull = nfull_ref[i]
    n_groups = lax.shift_right_logical(nfull, group_shift)
    n_pairs = lax.shift_right_logical(nfull, 1)
    carry = lax.fori_loop(0, n_groups, group_step, init)
    carry = lax.fori_loop(n_groups * (group // 2), n_pairs, pair_step, carry)
    carry = lax.fori_loop(2 * n_pairs, nfull, single_step, carry)
    carry = lax.fori_loop(nfull, nkv_ref[i], masked_step, carry)
    for h in range(n_sub):
        _, l, acc = carry[h]
        o_ref[h * sub:(h + 1) * sub, :] = _bf16((acc / l).T)


def _attention(q, k, vt, pos_col, pos_row, tq=512, tk=512, n_sub=2, group_shift=ATTN_GROUP_SHIFT):
    heads, s, _ = q.shape
    tq = min(tq, s)
    tk = min(tk, s)
    nfull, nkv = _kv_tile_ranges(pos_row.reshape(-1), tq, tk)
    return pl.pallas_call(
        functools.partial(_attn_kernel, tk=tk, n_sub=n_sub, group_shift=group_shift),
        out_shape=jax.ShapeDtypeStruct((s, heads * V_DIM), jnp.bfloat16),
        grid_spec=pltpu.PrefetchScalarGridSpec(
            num_scalar_prefetch=2,
            grid=(heads, s // tq),
            in_specs=[pl.BlockSpec((None, tq, QK_DIM), lambda h, i, nf, nk: (h, i, 0)),
                      pl.BlockSpec((None, s, QK_DIM), lambda h, i, nf, nk: (h, 0, 0)),
                      pl.BlockSpec((V_DIM, s), lambda h, i, nf, nk: (h, 0)),
                      pl.BlockSpec((s, 1), lambda h, i, nf, nk: (0, 0)),
                      pl.BlockSpec((1, tq), lambda h, i, nf, nk: (0, i))],
            out_specs=pl.BlockSpec((tq, V_DIM), lambda h, i, nf, nk: (i, h))),
        compiler_params=_cparams(("parallel", "arbitrary")),
        name="mla_attention",
    )(nfull, nkv, q, k, vt, pos_col, pos_row)


def _kv_tile_ranges(positions, tq, tk):
    cid = positions >> CHUNK_SHIFT
    qc = cid.reshape(-1, tq)
    kc = cid.reshape(-1, tk)
    qmin, qmax = jnp.min(qc, axis=1), jnp.max(qc, axis=1)
    kmin, kmax = jnp.min(kc, axis=1), jnp.max(kc, axis=1)
    vis = kmin[None, :] <= qmax[:, None]
    last = jnp.max(jnp.where(vis, jnp.arange(kmin.shape[0], dtype=jnp.int32)[None, :], -1), axis=1)
    full = (kmax[None, :] <= qmin[:, None]).astype(jnp.int32)
    nfull = jnp.sum(jnp.cumprod(full, axis=1), axis=1)
    return nfull.astype(jnp.int32), (last + 1).astype(jnp.int32)


def _shared_kv(hk, cos, sin, w_dkv, kv_a_norm_g, w_kr, w_uk, w_uv, k_norm_g):
    ckv = _mm(hk, _bf16(w_dkv), out_dtype=jnp.bfloat16, tm=1024, tn=w_dkv.shape[1],
              mode="rmsnorm", extra=kv_a_norm_g, name="mla_ckv")
    w_kr_pad = jnp.pad(_bf16(w_kr), ((0, 0), (0, LANES - ROPE_DIM)))
    kr_raw = _mm(hk, w_kr_pad, out_dtype=jnp.float32, tm=1024, tn=LANES, name="mla_krope")
    k = _kbuild(ckv, w_uk, kr_raw, k_norm_g, cos, sin)
    vt = _mm(ckv, _bf16(w_uv), out_dtype=jnp.bfloat16, tm=1024, tn=1024, mode="transpose",
             name="mla_v")
    return k, vt


def _mla_layer(x, h, k, vt, cos, sin, pos_col, pos_row, w_dq_all, q_a_norm_g, w_uq, q_norm_g, w_o_all, layer):
    qa = _mm(h, w_dq_all, layer=layer, out_dtype=jnp.bfloat16, tm=1024, tn=w_dq_all.shape[2], tk=2048,
             mode="rmsnorm", extra=q_a_norm_g, name="mla_qa")
    q = _qbuild(qa, w_uq, q_norm_g, cos, sin)
    o = _attention(q, k, vt, pos_col, pos_row)
    return _mm(o, w_o_all, layer=layer, out_dtype=jnp.float32, tm=1024, tn=512, mode="residual", extra=x,
               name="mla_out_proj")


def kernel(x, positions, g_mix, g_ffn, ssm_w_in, ssm_conv_w, ssm_conv_b, ssm_dt_bias, ssm_a_log, ssm_d, ssm_norm_g, ssm_w_out, kv_norm_g, w_dkv, kv_a_norm_g, w_kr, w_uk, w_uv, k_norm_g, q_w_dq, q_a_norm_g, q_w_uq, q_norm_g, attn_w_o, moe_w_group, moe_w_expert, moe_w_gate, moe_w_up, moe_w_down):
    bsz, seq, d = x.shape
    assert bsz == 1
    depth = g_mix.shape[0]
    n_a = ssm_w_in.shape[0]
    xs = x.reshape(seq, d)
    pos = positions.reshape(seq).astype(jnp.int32)
    pos_col = pos.reshape(seq, 1)
    pos_row = pos.reshape(1, seq)
    cos, sin = _rope_tables(pos_col)
    k_shared = vt_shared = None
    h = _prenorm(xs, g_mix[0])
    for layer in range(depth):
        if layer < n_a:
            i = layer
            xs = _mamba_layer(xs, h, ssm_w_in, ssm_conv_w[i], ssm_conv_b[i],
                              ssm_dt_bias[i], ssm_a_log[i], ssm_d[i], ssm_norm_g[i], ssm_w_out, i)
        else:
            j = layer - n_a
            xs = _mla_layer(xs, h, k_shared, vt_shared, cos, sin, pos_col, pos_row,
                            q_w_dq, q_a_norm_g[j], q_w_uq[j], q_norm_g[j], attn_w_o, j)
        next_gains = [g_mix[layer + 1]] if layer + 1 < depth else []
        if layer == n_a - 1:
            next_gains = next_gains + [kv_norm_g]
        xs, normed = _moe_layer(xs, g_ffn[layer], moe_w_group[layer], moe_w_expert[layer],
                                moe_w_gate, moe_w_up, moe_w_down, layer, next_gains)
        if layer + 1 < depth:
            h = normed[0]
        if layer == n_a - 1:
            k_shared, vt_shared = _shared_kv(normed[-1], cos, sin, w_dkv, kv_a_norm_g, w_kr,
                                             w_uk, w_uv, k_norm_g)
    return xs.reshape(bsz, seq, d)
```
